```python
import jax, jax.numpy as jnp
from jax import lax
import numpy as np

D_MODEL = 2048
BATCH = 2
SEQ = 4096
DEPTH = 2
DEC_BATCH = 128
DEC_SEQ = 1
PAST_LEN = 8192
PAGE_SIZE = 128

HEAD_DIM = 64
SCALE = HEAD_DIM ** -0.5
BLOCK = 128
SPAN = 128
CONV_WIDTH = 3
A_WIDTH = D_MODEL // 2
SWA_WINDOW = SPAN
SWA_Q_HEADS = D_MODEL // 128
SWA_KV_HEADS = SWA_Q_HEADS // 4
SWA_GROUP = SWA_Q_HEADS // SWA_KV_HEADS
SWA_Q = SWA_Q_HEADS * HEAD_DIM
SWA_KV = SWA_KV_HEADS * HEAD_DIM
DIL_PAIRS = ((128, 1), (512, 4), (2048, 16))
N_DIL = 3
DIL_HEADS = D_MODEL // 256
DIL_OUT = DIL_HEADS * HEAD_DIM
DIL_QKV = N_DIL * DIL_OUT
N_BRANCH = 3
N_IN = 3 * A_WIDTH + SWA_Q + 2 * SWA_KV + 3 * DIL_QKV + N_BRANCH * D_MODEL
D_FF = ((8 * D_MODEL // 3 + 255) // 256) * 256
N_SUB = 3
ALPHA = (2.0 * DEPTH) ** 0.25
BETA = (8.0 * DEPTH) ** -0.25
LN_EPS = 1e-5

kernel_name = 'hybrid_gatedconv_swa_dilated_step'


def layer_norm(x, g, b):
    xf = x.astype(jnp.float32)
    mu = jnp.mean(xf, axis=-1, keepdims=True)
    var = jnp.mean(jnp.square(xf - mu), axis=-1, keepdims=True)
    return ((xf - mu) * lax.rsqrt(var + LN_EPS) * g + b).astype(x.dtype)


def adaln(c, w, b):
    m = jax.nn.silu(c) @ w + b
    return m.reshape(c.shape[0], 1, 3 * N_SUB, D_MODEL)


def modulate(x, mod, i):
    return x * (1 + mod[:, :, 3 * i + 1]) + mod[:, :, 3 * i]


def post_norm(x, mod, i, res_w, y, g, b):
    return layer_norm(ALPHA * x + res_w * mod[:, :, 3 * i + 2] * y, g, b)


def swiglu(h, wg, wu, wd):
    return (jax.nn.silu(h @ wg) * (h @ wu)) @ wd


def sink_softmax(s, valid, sink):
    s = jnp.where(valid, s, -jnp.inf)
    m = jnp.max(s, axis=-1, keepdims=True)
    if sink is not None:
        m = jnp.maximum(m, sink)
    e = jnp.exp(s - m)
    den = jnp.sum(e, axis=-1, keepdims=True)
    if sink is not None:
        den = den + jnp.exp(sink - m)
    return e / den, (m + jnp.log(den))[..., 0]


def banded_window_attention(q, k, v, sink):
    n, seq_len, hk, g, dh = q.shape
    nb = -(-seq_len // BLOCK)
    pad = nb * BLOCK - seq_len
    qb = jnp.pad(q, ((0, 0), (0, pad), (0, 0), (0, 0), (0, 0))).reshape(n, nb, BLOCK, hk, g, dh)
    kv_pad = ((0, 0), (BLOCK, pad), (0, 0), (0, 0))
    kb = jnp.pad(k, kv_pad).reshape(n, nb + 1, BLOCK, hk, dh)
    vb = jnp.pad(v, kv_pad).reshape(n, nb + 1, BLOCK, hk, dh)
    kw = jnp.concatenate([kb[:, :-1], kb[:, 1:]], axis=2)
    vw = jnp.concatenate([vb[:, :-1], vb[:, 1:]], axis=2)
    s = jnp.einsum('nbqhgd,nbkhd->nbhgqk', qb, kw, preferred_element_type=jnp.float32) * SCALE
    qpos = jnp.arange(nb)[:, None] * BLOCK + jnp.arange(BLOCK)[None, :]
    kpos = jnp.arange(nb)[:, None] * BLOCK + jnp.arange(2 * BLOCK)[None, :] - BLOCK
    dist = qpos[:, :, None] - kpos[:, None, :]
    valid = (dist >= 0) & (dist <= SPAN) & (kpos[:, None, :] >= 0)
    valid = valid[None, :, None, None]
    sk = None if sink is None else sink.astype(jnp.float32)[None, None, :, :, None, None]
    p, lse = sink_softmax(s, valid, sk)
    o = jnp.einsum('nbhgqk,nbkhd->nbqhgd', p, vw.astype(jnp.float32))
    o = o.reshape(n, nb * BLOCK, hk, g, dh)[:, :seq_len]
    lse = lse.transpose(0, 1, 4, 2, 3).reshape(n, nb * BLOCK, hk, g)[:, :seq_len]
    return o, lse


def to_residues(t, r):
    b, s = t.shape[:2]
    t = jnp.swapaxes(t.reshape((b, s // r, r) + t.shape[2:]), 1, 2)
    return t.reshape((b * r, s // r) + t.shape[3:])


def from_residues(t, r, b):
    n, sub = t.shape[:2]
    t = jnp.swapaxes(t.reshape((b, r, sub) + t.shape[2:]), 1, 2)
    return t.reshape((b, sub * r) + t.shape[3:])


def combine_dilations(outs, lses):
    w = jax.nn.softmax(jnp.stack(lses, 0), axis=0)
    o = jnp.sum(w[..., None] * jnp.stack(outs, 0), axis=0)
    return o.reshape(o.shape[:2] + (DIL_OUT,))


def short_conv(u_ext, conv_w, n_out):
    return sum(conv_w[j] * u_ext[:, j:j + n_out] for j in range(CONV_WIDTH))


def project_in(h, w_in):
    sizes = (A_WIDTH,) * 3 + (SWA_Q, SWA_KV, SWA_KV) + (DIL_QKV,) * 3 + (D_MODEL,) * N_BRANCH
    cuts = np.cumsum(sizes)[:-1].tolist()
    return jnp.split(h @ w_in, cuts, axis=-1)


def merge_branches(y_a, y_b, y_c, g_a, g_b, g_c, w_br_a, w_br_b, w_br_c, w_out):
    m = (jax.nn.sigmoid(g_a) * (y_a @ w_br_a) + jax.nn.sigmoid(g_b) * (y_b @ w_br_b)
         + jax.nn.sigmoid(g_c) * (y_c @ w_br_c))
    return m @ w_out


def mixers_prompt(h, w_in, conv_w, sink, w_br_a, w_br_b, w_br_c, w_out):
    bsz, s_len, _ = h.shape
    a_x, a_b, a_c, b_q, b_k, b_v, c_q, c_k, c_v, g_a, g_b, g_c = project_in(h, w_in)
    u = a_c * a_x
    u_ext = jnp.pad(u, ((0, 0), (CONV_WIDTH - 1, 0), (0, 0)))
    y_a = a_b * short_conv(u_ext, conv_w, s_len)
    q = b_q.reshape(bsz, s_len, SWA_KV_HEADS, SWA_GROUP, HEAD_DIM)
    k = b_k.reshape(bsz, s_len, SWA_KV_HEADS, HEAD_DIM)
    v = b_v.reshape(bsz, s_len, SWA_KV_HEADS, HEAD_DIM)
    o, _ = banded_window_attention(q, k, v, sink.reshape(SWA_KV_HEADS, SWA_GROUP))
    y_b = o.reshape(bsz, s_len, SWA_Q).astype(h.dtype)
    cq = c_q.reshape(bsz, s_len, N_DIL, DIL_HEADS, HEAD_DIM)
    ck = c_k.reshape(bsz, s_len, N_DIL, DIL_HEADS, HEAD_DIM)
    cv = c_v.reshape(bsz, s_len, N_DIL, DIL_HEADS, HEAD_DIM)
    outs, lses, dil_states = [], [], []
    for gi, (win, r) in enumerate(DIL_PAIRS):
        og, lg = banded_window_attention(to_residues(cq[:, :, gi], r)[:, :, :, None],
                                         to_residues(ck[:, :, gi], r), to_residues(cv[:, :, gi], r), None)
        outs.append(from_residues(og[:, :, :, 0], r, bsz))
        lses.append(from_residues(lg[..., 0], r, bsz))
        rows = min(win, s_len)
        dil_states += [ck[:, -rows:, gi], cv[:, -rows:, gi]]
    y_c = combine_dilations(outs, lses).astype(h.dtype)
    y = merge_branches(y_a, y_b, y_c, g_a, g_b, g_c, w_br_a, w_br_b, w_br_c, w_out)
    rows = min(SWA_WINDOW, s_len)
    states = [u[:, -(CONV_WIDTH - 1):], k[:, -rows:], v[:, -rows:]] + dil_states
    return y, states


def mixers_sample(h, conv_state, swa_k, swa_v, d0k, d0v, d1k, d1v, d2k, d2v,
                  w_in, conv_w, sink, w_br_a, w_br_b, w_br_c, w_out):
    bsz, t_len, _ = h.shape
    a_x, a_b, a_c, b_q, b_k, b_v, c_q, c_k, c_v, g_a, g_b, g_c = project_in(h, w_in)
    u = a_c * a_x
    u_cat = jnp.concatenate([conv_state.astype(u.dtype), u], axis=1)
    y_a = a_b * short_conv(u_cat, conv_w, t_len)
    q = b_q.reshape(bsz, t_len, SWA_KV_HEADS, SWA_GROUP, HEAD_DIM)
    k_cat = jnp.concatenate([swa_k.astype(h.dtype), b_k.reshape(bsz, t_len, SWA_KV_HEADS, HEAD_DIM)], axis=1)
    v_cat = jnp.concatenate([swa_v.astype(h.dtype), b_v.reshape(bsz, t_len, SWA_KV_HEADS, HEAD_DIM)], axis=1)
    wb = swa_k.shape[1]
    dist = (wb + jnp.arange(t_len))[:, None] - jnp.arange(wb + t_len)[None, :]
    valid = (dist >= 0) & (dist <= SWA_WINDOW)
    s = jnp.einsum('bthgd,bkhd->bhgtk', q, k_cat, preferred_element_type=jnp.float32) * SCALE
    sk = sink.reshape(SWA_KV_HEADS, SWA_GROUP).astype(jnp.float32)[None, :, :, None, None]
    p, _ = sink_softmax(s, valid[None, None, None], sk)
    y_b = jnp.einsum('bhgtk,bkhd->bthgd', p, v_cat.astype(jnp.float32)).reshape(bsz, t_len, SWA_Q).astype(h.dtype)
    cq = c_q.reshape(bsz, t_len, N_DIL, DIL_HEADS, HEAD_DIM)
    ck = c_k.reshape(bsz, t_len, N_DIL, DIL_HEADS, HEAD_DIM)
    cv = c_v.reshape(bsz, t_len, N_DIL, DIL_HEADS, HEAD_DIM)
    outs, lses, dil_states = [], [], []
    for gi, ((win, r), (kb, vb)) in enumerate(zip(DIL_PAIRS, ((d0k, d0v), (d1k, d1v), (d2k, d2v)))):
        kc = jnp.concatenate([kb.astype(h.dtype), ck[:, :, gi]], axis=1)
        vc = jnp.concatenate([vb.astype(h.dtype), cv[:, :, gi]], axis=1)
        wbg = kb.shape[1]
        idx = (wbg + jnp.arange(t_len))[:, None] - r * jnp.arange(SPAN + 1)[None, :]
        gidx = jnp.maximum(idx, 0)
        kg = kc[:, gidx]
        vg = vc[:, gidx]
        s = jnp.einsum('bthd,btjhd->bhtj', cq[:, :, gi], kg, preferred_element_type=jnp.float32) * SCALE
        p, lg = sink_softmax(s, (idx >= 0)[None, None], None)
        outs.append(jnp.einsum('bhtj,btjhd->bthd', p, vg.astype(jnp.float32)))
        lses.append(jnp.swapaxes(lg, 1, 2))
        dil_states += [kc[:, -wbg:], vc[:, -wbg:]]
    y_c = combine_dilations(outs, lses).astype(h.dtype)
    y = merge_branches(y_a, y_b, y_c, g_a, g_b, g_c, w_br_a, w_br_b, w_br_c, w_out)
    states = [u_cat[:, -(CONV_WIDTH - 1):], k_cat[:, -wb:], v_cat[:, -wb:]] + dil_states
    return y, states


def run_layer(x, c, mixer, w_ada_l, b_ada_l, ln_g_l, ln_b_l, wg, wu, wd):
    mod = adaln(c, w_ada_l, b_ada_l)
    x = post_norm(x, mod, 0, 0.5, swiglu(modulate(x, mod, 0), wg[0], wu[0], wd[0]), ln_g_l[0], ln_b_l[0])
    y, states = mixer(modulate(x, mod, 1))
    x = post_norm(x, mod, 1, 1.0, y, ln_g_l[1], ln_b_l[1])
    x = post_norm(x, mod, 2, 0.5, swiglu(modulate(x, mod, 2), wg[1], wu[1], wd[1]), ln_g_l[2], ln_b_l[2])
    return x, states


def setup_inputs(seed: int = 0) -> dict:
    key = jax.random.key(seed)
    ks = jax.random.split(key, 32)
    f32 = jnp.float32

    def nrm(k, shape, scale):
        return jax.random.normal(k, shape, f32) * scale

    wb_swa = min(SWA_WINDOW, PAST_LEN)
    wb_d = [min(w, PAST_LEN) for w, _ in DIL_PAIRS]
    kvs = (DEPTH, DEC_BATCH)
    return {
        'x_prompt': nrm(ks[0], (BATCH, SEQ, D_MODEL), 1.0),
        'x_sample': nrm(ks[1], (DEC_BATCH, DEC_SEQ, D_MODEL), 1.0),
        'state_conv': nrm(ks[2], kvs + (CONV_WIDTH - 1, A_WIDTH), 1.0),
        'cache_swa_k': nrm(ks[3], kvs + (wb_swa, SWA_KV_HEADS, HEAD_DIM), 1.0),
        'cache_swa_v': nrm(ks[4], kvs + (wb_swa, SWA_KV_HEADS, HEAD_DIM), 1.0),
        'cache_dil0_k': nrm(ks[5], kvs + (wb_d[0], DIL_HEADS, HEAD_DIM), 1.0),
        'cache_dil0_v': nrm(ks[6], kvs + (wb_d[0], DIL_HEADS, HEAD_DIM), 1.0),
        'cache_dil1_k': nrm(ks[7], kvs + (wb_d[1], DIL_HEADS, HEAD_DIM), 1.0),
        'cache_dil1_v': nrm(ks[8], kvs + (wb_d[1], DIL_HEADS, HEAD_DIM), 1.0),
        'cache_dil2_k': nrm(ks[9], kvs + (wb_d[2], DIL_HEADS, HEAD_DIM), 1.0),
        'cache_dil2_v': nrm(ks[10], kvs + (wb_d[2], DIL_HEADS, HEAD_DIM), 1.0),
        'c_prompt': nrm(ks[11], (BATCH, D_MODEL), 1.0),
        'c_sample': nrm(ks[12], (DEC_BATCH, D_MODEL), 1.0),
        'w_ada': nrm(ks[13], (DEPTH, D_MODEL, 3 * N_SUB * D_MODEL), 0.5 * D_MODEL ** -0.5),
        'b_ada': nrm(ks[14], (DEPTH, 3 * N_SUB * D_MODEL), 0.01),
        'ln_g': 1.0 + nrm(ks[15], (DEPTH, N_SUB, D_MODEL), 0.02),
        'ln_b': nrm(ks[16], (DEPTH, N_SUB, D_MODEL), 0.02),
        'ffn_w_gate': nrm(ks[17], (DEPTH, 2, D_MODEL, D_FF), D_MODEL ** -0.5),
        'ffn_w_up': nrm(ks[18], (DEPTH, 2, D_MODEL, D_FF), D_MODEL ** -0.5),
        'ffn_w_down': nrm(ks[19], (DEPTH, 2, D_FF, D_MODEL), BETA * D_FF ** -0.5),
        'w_in': nrm(ks[20], (DEPTH, D_MODEL, N_IN), D_MODEL ** -0.5),
        'conv_w': nrm(ks[21], (DEPTH, CONV_WIDTH, A_WIDTH), CONV_WIDTH ** -0.5),
        'attn_sink': nrm(ks[22], (DEPTH, SWA_Q_HEADS), 1.0),
        'w_br_a': nrm(ks[23], (DEPTH, A_WIDTH, D_MODEL), BETA * A_WIDTH ** -0.5),
        'w_br_b': nrm(ks[24], (DEPTH, SWA_Q, D_MODEL), BETA * SWA_Q ** -0.5),
        'w_br_c': nrm(ks[25], (DEPTH, DIL_OUT, D_MODEL), BETA * DIL_OUT ** -0.5),
        'w_out': nrm(ks[26], (DEPTH, D_MODEL, D_MODEL), BETA * D_MODEL ** -0.5),
    }


def reference(x_prompt, x_sample, state_conv, cache_swa_k, cache_swa_v,
              cache_dil0_k, cache_dil0_v, cache_dil1_k, cache_dil1_v, cache_dil2_k, cache_dil2_v,
              c_prompt, c_sample, w_ada, b_ada, ln_g, ln_b, ffn_w_gate, ffn_w_up, ffn_w_down,
              w_in, conv_w, attn_sink, w_br_a, w_br_b, w_br_c, w_out):
    xp, xs = x_prompt, x_sample
    new_p = [[] for _ in range(9)]
    new_s = [[] for _ in range(9)]
    for l in range(DEPTH):
        wl = (w_in[l], conv_w[l], attn_sink[l], w_br_a[l], w_br_b[l], w_br_c[l], w_out[l])
        nf = (w_ada[l], b_ada[l], ln_g[l], ln_b[l], ffn_w_gate[l], ffn_w_up[l], ffn_w_down[l])
        past = (state_conv[l], cache_swa_k[l], cache_swa_v[l], cache_dil0_k[l], cache_dil0_v[l],
                cache_dil1_k[l], cache_dil1_v[l], cache_dil2_k[l], cache_dil2_v[l])
        xp, st_p = run_layer(xp, c_prompt, lambda h: mixers_prompt(h, *wl), *nf)
        xs, st_s = run_layer(xs, c_sample, lambda h: mixers_sample(h, *past, *wl), *nf)
        for acc, s in zip(new_p, st_p):
            acc.append(s)
        for acc, s in zip(new_s, st_s):
            acc.append(s)
    P = [jnp.stack(a, 0) for a in new_p]
    S = [jnp.stack(a, 0) for a in new_s]
    return (xp, xs, P[0], S[0], P[1], S[1], P[2], S[2], P[3], S[3], P[4], S[4],
            P[5], S[5], P[6], S[6], P[7], S[7], P[8], S[8])
```

```python
import functools

import jax
import jax.numpy as jnp
from jax import lax
from jax.experimental import pallas as pl
from jax.experimental.pallas import tpu as pltpu

HEAD_DIM = 64
SPAN = 128
CONV_WIDTH = 3
N_SUB = 3
DILATIONS = (1, 4, 16)
LN_EPS = 1e-5
SCALE = HEAD_DIM ** -0.5
VMEM_LIMIT_BYTES = 56 * 1024 * 1024

F32 = jnp.float32
BF16 = jnp.bfloat16


def _params(*sem):
    return pltpu.CompilerParams(dimension_semantics=sem, vmem_limit_bytes=VMEM_LIMIT_BYTES)


def _largest_tile(n, cap):
    return max(t for t in range(128, cap + 1, 128) if n % t == 0)


def _dot(a, b):
    return jnp.dot(a, b, preferred_element_type=F32)


def _dot_nt(a, b):
    return lax.dot_general(a, b, (((1,), (1,)), ((), ())), preferred_element_type=F32)


def _layer_norm(z, g, b):
    mu = jnp.mean(z, axis=-1, keepdims=True)
    zc = z - mu
    var = jnp.mean(zc * zc, axis=-1, keepdims=True)
    return zc * lax.rsqrt(var + LN_EPS) * g + b


def _silu(x):
    return x * jax.nn.sigmoid(x)


def _ada_kernel(c_ref, w_ref, b_ref, o_ref):
    a = _silu(c_ref[...]).astype(BF16)
    o_ref[...] = _dot(a, w_ref[...].astype(BF16)) + b_ref[...]


def _ada(c_all, w_ada, b_ada, tn):
    depth, d, n = w_ada.shape
    r = c_all.shape[0]
    return pl.pallas_call(
        _ada_kernel,
        grid=(depth, n // tn),
        in_specs=[pl.BlockSpec((r, d), lambda l, j: (0, 0)),
                  pl.BlockSpec((None, d, tn), lambda l, j: (l, 0, j)),
                  pl.BlockSpec((None, 1, tn), lambda l, j: (l, 0, j))],
        out_specs=pl.BlockSpec((None, r, tn), lambda l, j: (l, 0, j)),
        out_shape=jax.ShapeDtypeStruct((depth, r, n), F32),
        compiler_params=_params("arbitrary", "arbitrary"),
        name="adaln",
    )(c_all, w_ada, b_ada.reshape(depth, 1, n))


def _ffn_kernel(x_ref, sh_ref, sc_ref, gt_ref, wg_ref, wu_ref, wd_ref, lg_ref, lb_ref,
                o_ref, h_ref, acc_ref, *, alpha, res_w):
    f = pl.program_id(1)

    @pl.when(f == 0)
    def _():
        h_ref[...] = (x_ref[...] * (1.0 + sc_ref[...]) + sh_ref[...]).astype(BF16)
        acc_ref[...] = jnp.zeros_like(acc_ref)

    h = h_ref[...]
    g = _dot(h, wg_ref[...])
    u = _dot(h, wu_ref[...])
    a = (_silu(g) * u).astype(BF16)
    acc_ref[...] += _dot(a, wd_ref[...])

    @pl.when(f == pl.num_programs(1) - 1)
    def _():
        z = alpha * x_ref[...] + res_w * gt_ref[...] * acc_ref[...]
        o_ref[...] = _layer_norm(z, lg_ref[...], lb_ref[...])


def _mod_spec(rows_per_group, d, tiles_per_group, col):
    return pl.BlockSpec((None, rows_per_group, d),
                        lambda i, j, t=tiles_per_group, c=col: (i // t, 0, c))


def _ffn(x, mod, sub, wg, wu, wd, ln_g, ln_b, *, tm, tf, alpha, res_w):
    m, d = x.shape
    f_dim = wg.shape[1]
    groups, rpg, _ = mod.shape
    tpg = (m // groups) // tm
    ms = functools.partial(_mod_spec, rpg, d, tpg)
    return pl.pallas_call(
        functools.partial(_ffn_kernel, alpha=alpha, res_w=res_w),
        grid=(m // tm, f_dim // tf),
        in_specs=[pl.BlockSpec((tm, d), lambda i, j: (i, 0)),
                  ms(3 * sub), ms(3 * sub + 1), ms(3 * sub + 2),
                  pl.BlockSpec((d, tf), lambda i, j: (0, j)),
                  pl.BlockSpec((d, tf), lambda i, j: (0, j)),
                  pl.BlockSpec((tf, d), lambda i, j: (j, 0)),
                  pl.BlockSpec((1, d), lambda i, j: (0, 0)),
                  pl.BlockSpec((1, d), lambda i, j: (0, 0))],
        out_specs=pl.BlockSpec((tm, d), lambda i, j: (i, 0)),
        out_shape=jax.ShapeDtypeStruct((m, d), F32),
        scratch_shapes=[pltpu.VMEM((tm, d), BF16), pltpu.VMEM((tm, d), F32)],
        compiler_params=_params("arbitrary", "arbitrary"),
        name="ffn",
    )(x, mod, mod, mod, wg, wu, wd, ln_g.reshape(1, d), ln_b.reshape(1, d))


def _proj_kernel(x_ref, sh_ref, sc_ref, w_ref, o_ref, h_ref):
    @pl.when(pl.program_id(1) == 0)
    def _():
        h_ref[...] = (x_ref[...] * (1.0 + sc_ref[...]) + sh_ref[...]).astype(BF16)

    o_ref[...] = _dot(h_ref[...], w_ref[...])


def _proj(x, mod, w_in, *, tm, tn):
    m, d = x.shape
    n = w_in.shape[1]
    groups, rpg, _ = mod.shape
    tpg = (m // groups) // tm
    ms = functools.partial(_mod_spec, rpg, d, tpg)
    return pl.pallas_call(
        _proj_kernel,
        grid=(m // tm, n // tn),
        in_specs=[pl.BlockSpec((tm, d), lambda i, j: (i, 0)),
                  ms(3), ms(4),
                  pl.BlockSpec((d, tn), lambda i, j: (0, j))],
        out_specs=pl.BlockSpec((tm, tn), lambda i, j: (i, j)),
        out_shape=jax.ShapeDtypeStruct((m, n), F32),
        scratch_shapes=[pltpu.VMEM((tm, d), BF16)],
        compiler_params=_params("arbitrary", "arbitrary"),
        name="proj_in",
    )(x, mod, mod, w_in)


def _band_kernel(*refs, n_kv, group, has_sink, has_lse):
    q_ref, kp_ref, kc_ref, vp_ref, vc_ref = refs[:5]
    pos = 5
    sink_ref = None
    if has_sink:
        sink_ref = refs[pos]
        pos += 1
    o_ref = refs[pos]
    lse_ref = refs[pos + 1] if has_lse else None

    j = pl.program_id(2)
    k = jnp.concatenate([kp_ref[...], kc_ref[...]], axis=0).astype(BF16)
    v = jnp.concatenate([vp_ref[...], vc_ref[...]], axis=0).astype(BF16)
    qi = lax.broadcasted_iota(jnp.int32, (SPAN, 2 * SPAN), 0)
    ki = lax.broadcasted_iota(jnp.int32, (SPAN, 2 * SPAN), 1)
    dist = qi + SPAN - ki
    valid = (dist >= 0) & (dist <= SPAN) & ((ki >= SPAN) | (j > 0))
    for h in range(n_kv):
        kh = k[:, h * HEAD_DIM:(h + 1) * HEAD_DIM]
        vh = v[:, h * HEAD_DIM:(h + 1) * HEAD_DIM]
        for g in range(group):
            hq = h * group + g
            cols = slice(hq * HEAD_DIM, (hq + 1) * HEAD_DIM)
            s = _dot_nt(q_ref[:, cols].astype(BF16), kh) * SCALE
            s = jnp.where(valid, s, -jnp.inf)
            m = jnp.max(s, axis=-1, keepdims=True)
            if has_sink:
                sk = sink_ref[:, hq:hq + 1]
                m = jnp.maximum(m, sk)
            e = jnp.exp(s - m)
            den = jnp.sum(e, axis=-1, keepdims=True)
            if has_sink:
                den = den + jnp.exp(sk - m)
            p = e / den
            o_ref[:, cols] = _dot(p.astype(BF16), vh)
            if has_lse:
                lse_ref[:, cols] = jnp.broadcast_to(m + jnp.log(den), (SPAN, HEAD_DIM))


def _band_attention(p3, r, q_off, k_off, v_off, n_kv, group, sink):
    b, s, n = p3.shape
    qw, kw = n_kv * group * HEAD_DIM, n_kv * HEAD_DIM
    sub = s // r
    nb = sub // SPAN
    pv = p3.reshape(b, sub, r * n)
    qspec = pl.BlockSpec((None, SPAN, qw), lambda bi, ri, j: (bi, j, ri * (n // qw) + q_off // qw))

    def kvspec(off, prev):
        if prev:
            return pl.BlockSpec((None, SPAN, kw),
                                lambda bi, ri, j: (bi, jnp.maximum(j - 1, 0), ri * (n // kw) + off // kw))
        return pl.BlockSpec((None, SPAN, kw), lambda bi, ri, j: (bi, j, ri * (n // kw) + off // kw))

    in_specs = [qspec, kvspec(k_off, True), kvspec(k_off, False), kvspec(v_off, True), kvspec(v_off, False)]
    args = [pv] * 5
    has_sink = sink is not None
    if has_sink:
        in_specs.append(pl.BlockSpec((1, qw // HEAD_DIM), lambda bi, ri, j: (0, 0)))
        args.append(sink.reshape(1, -1))
    ospec = pl.BlockSpec((None, SPAN, qw), lambda bi, ri, j: (bi, j, ri))
    oshape = jax.ShapeDtypeStruct((b, sub, r * qw), F32)
    has_lse = not has_sink
    out = pl.pallas_call(
        functools.partial(_band_kernel, n_kv=n_kv, group=group, has_sink=has_sink, has_lse=has_lse),
        grid=(b, r, nb),
        in_specs=in_specs,
        out_specs=[ospec, ospec] if has_lse else ospec,
        out_shape=[oshape, oshape] if has_lse else oshape,
        compiler_params=_params("arbitrary", "arbitrary", "arbitrary"),
        name=f"band_attn_r{r}",
    )(*args)
    if has_lse:
        return out[0].reshape(b, s, qw), out[1].reshape(b, s, qw)
    return out.reshape(b, s, qw)


def _conv_kernel(ax_ref, ab_ref, ac_ref, axp_ref, acp_ref, w_ref, y_ref, tail_ref, ue_ref, *, tm):
    i = pl.program_id(1)
    u = ac_ref[...] * ax_ref[...]
    halo = acp_ref[...] * axp_ref[...]
    ue_ref[0:8, :] = jnp.where(i > 0, halo, 0.0)
    ue_ref[8:, :] = u
    u1 = ue_ref[pl.ds(7, tm), :]
    u2 = ue_ref[pl.ds(6, tm), :]
    y_ref[...] = ab_ref[...] * (w_ref[0:1, :] * u2 + w_ref[1:2, :] * u1 + w_ref[2:3, :] * u)
    tail_ref[...] = u[tm - 8:, :]


def _conv_prompt(p3, conv_w, a_w, tm):
    b, s, n = p3.shape
    cur = lambda c: pl.BlockSpec((None, tm, a_w), lambda bi, i, c=c: (bi, i, c))
    prev = lambda c: pl.BlockSpec((None, 8, a_w),
                                  lambda bi, i, c=c: (bi, jnp.maximum(i * (tm // 8) - 1, 0), c))
    return pl.pallas_call(
        functools.partial(_conv_kernel, tm=tm),
        grid=(b, s // tm),
        in_specs=[cur(0), cur(1), cur(2), prev(0), prev(2),
                  pl.BlockSpec((CONV_WIDTH, a_w), lambda bi, i: (0, 0))],
        out_specs=[pl.BlockSpec((None, tm, a_w), lambda bi, i: (bi, i, 0)),
                   pl.BlockSpec((None, 8, a_w), lambda bi, i: (bi, 0, 0))],
        out_shape=[jax.ShapeDtypeStruct((b, s, a_w), F32), jax.ShapeDtypeStruct((b, 8, a_w), F32)],
        scratch_shapes=[pltpu.VMEM((tm + 8, a_w), F32)],
        compiler_params=_params("arbitrary", "arbitrary"),
        name="short_conv",
    )(p3, p3, p3, p3, p3, conv_w)


def _conv_step_kernel(ax_ref, ab_ref, ac_ref, st_ref, cw_ref, ya_ref, st_out_ref):
    a_w = ax_ref.shape[1]
    u = ac_ref[...] * ax_ref[...]
    s0 = st_ref[:, 0:a_w]
    s1 = st_ref[:, a_w:2 * a_w]
    ya_ref[...] = ab_ref[...] * (cw_ref[0:1, :] * s0 + cw_ref[1:2, :] * s1 + cw_ref[2:3, :] * u)
    st_out_ref[:, 0:a_w] = s1
    st_out_ref[:, a_w:2 * a_w] = u


def _conv_step(ps, layer, conv_state2, conv_w, a_w):
    bd = ps.shape[0]
    col = lambda c: pl.BlockSpec((bd, a_w), lambda i, c=c: (0, c))
    return pl.pallas_call(
        _conv_step_kernel,
        grid=(1,),
        in_specs=[col(0), col(1), col(2),
                  pl.BlockSpec((None, bd, 2 * a_w), lambda i: (layer, 0, 0)),
                  pl.BlockSpec((CONV_WIDTH, a_w), lambda i: (0, 0))],
        out_specs=[pl.BlockSpec((bd, a_w), lambda i: (0, 0)), pl.BlockSpec((bd, 2 * a_w), lambda i: (0, 0))],
        out_shape=[jax.ShapeDtypeStruct((bd, a_w), F32), jax.ShapeDtypeStruct((bd, 2 * a_w), F32)],
        compiler_params=_params("arbitrary"),
        name="conv_step",
    )(ps, ps, ps, conv_state2, conv_w)


def _pick_column(x, onehot):
    return jnp.sum(jnp.where(onehot, x, 0.0), axis=1, keepdims=True)


def _dec_kernel(*refs, bb, n_kv, group, dil, has_sink, aliased):
    qt_ref, knt_ref, vnt_ref = refs[:3]
    pos = 3
    sink_ref = None
    if has_sink:
        sink_ref = refs[pos]
        pos += 1
    k_ref, v_ref = refs[pos:pos + 2]
    pos += 4 if aliased else 2
    ko_ref, vo_ref, ot_ref, lt_ref = refs[pos:pos + 4]

    step = pl.program_id(0)
    w = k_ref.shape[-1]
    bd = qt_ref.shape[1]

    @pl.when(step == 0)
    def _():
        ot_ref[...] = jnp.zeros_like(ot_ref)
        lt_ref[...] = jnp.zeros_like(lt_ref)

    lane_b = lax.broadcasted_iota(jnp.int32, (1, bd), 1)
    lane_w = lax.broadcasted_iota(jnp.int32, (1, w), 1)
    key_ok = (lane_w % dil) == 0
    last = lane_w == w - 1

    def body(i, carry):
        onehot = lane_b == step * bb + i
        q_all = _pick_column(qt_ref[...], onehot)
        kn_all = _pick_column(knt_ref[...], onehot)
        vn_all = _pick_column(vnt_ref[...], onehot)
        for h in range(n_kv):
            rows = slice(h * HEAD_DIM, (h + 1) * HEAD_DIM)
            k = k_ref[i, h]
            v = v_ref[i, h]
            kn = kn_all[rows]
            vn = vn_all[rows]
            ko_ref[i, h] = jnp.where(last, kn, pltpu.roll(k, w - 1, 1))
            vo_ref[i, h] = jnp.where(last, vn, pltpu.roll(v, w - 1, 1))
            for g in range(group):
                hq = h * group + g
                qrows = slice(hq * HEAD_DIM, (hq + 1) * HEAD_DIM)
                q = q_all[qrows]
                s = jnp.sum(k * q, axis=0, keepdims=True) * SCALE
                s = jnp.where(key_ok, s, -jnp.inf)
                s_new = jnp.sum(kn * q, axis=0, keepdims=True) * SCALE
                m = jnp.maximum(jnp.max(s, axis=1, keepdims=True), s_new)
                if has_sink:
                    sk = sink_ref[hq:hq + 1, :]
                    m = jnp.maximum(m, sk)
                e = jnp.exp(s - m)
                e_new = jnp.exp(s_new - m)
                den = jnp.sum(e, axis=1, keepdims=True) + e_new
                if has_sink:
                    den = den + jnp.exp(sk - m)
                o = (jnp.sum(v * e, axis=1, keepdims=True) + e_new * vn) / den
                ot_ref[qrows, :] = jnp.where(onehot, o, ot_ref[qrows, :])
                lt_ref[qrows, :] = jnp.where(onehot, m + jnp.log(den), lt_ref[qrows, :])
        return carry

    lax.fori_loop(0, bb, body, 0)


def _decode_group(qt, knt, vnt, sink, k_cache, v_cache, k_acc, v_acc, layer, dil, bb):
    depth, bd, n_kv, _, w = k_cache.shape
    hq = qt.shape[0] // HEAD_DIM
    full = lambda a: pl.BlockSpec(a.shape, lambda i: (0, 0))
    cache = pl.BlockSpec((None, bb, n_kv, HEAD_DIM, w), lambda i: (layer, i, 0, 0, 0))
    in_specs = [full(qt), full(knt), full(vnt)]
    args = [qt, knt, vnt]
    if sink is not None:
        in_specs.append(pl.BlockSpec((hq, 1), lambda i: (0, 0)))
        args.append(sink.reshape(hq, 1))
    in_specs += [cache, cache]
    args += [k_cache, v_cache]
    aliases = {}
    if k_acc is not None:
        aliases = {len(args): 0, len(args) + 1: 1}
        in_specs += [pl.BlockSpec(memory_space=pl.ANY), pl.BlockSpec(memory_space=pl.ANY)]
        args += [k_acc, v_acc]
    small = jax.ShapeDtypeStruct(qt.shape, F32)
    return pl.pallas_call(
        functools.partial(_dec_kernel, bb=bb, n_kv=n_kv, group=hq // n_kv, dil=dil,
                          has_sink=sink is not None, aliased=k_acc is not None),
        grid=(bd // bb,),
        in_specs=in_specs,
        out_specs=[cache, cache, full(qt), full(qt)],
        out_shape=[jax.ShapeDtypeStruct(k_cache.shape, F32), jax.ShapeDtypeStruct(v_cache.shape, F32), small, small],
        input_output_aliases=aliases,
        compiler_params=_params("arbitrary"),
        name=f"decode_w{w}",
    )(*args)


def _merge_kernel(*refs, alpha, n_lse, d):
    x_ref, gt_ref, ya_ref, yb_ref = refs[:4]
    pos = 4
    if n_lse:
        o_refs = refs[pos:pos + n_lse]
        l_refs = refs[pos + n_lse:pos + 2 * n_lse]
        pos += 2 * n_lse
        ls = [r[...] for r in l_refs]
        mx = functools.reduce(jnp.maximum, ls)
        ws = [jnp.exp(l - mx) for l in ls]
        tot = functools.reduce(lambda a, b: a + b, ws)
        yc = functools.reduce(lambda a, b: a + b, [(w / tot) * o[...] for w, o in zip(ws, o_refs)])
    else:
        yc = refs[pos][...]
        pos += 1
    g1_ref, g2_ref, wa_ref, wb_ref, wc_ref, wo_ref, lg_ref, lb_ref, out_ref = refs[pos:]
    hw = d // 2
    ga = g1_ref[:, 0:d]
    gb = jnp.concatenate([g1_ref[:, d:d + hw], g2_ref[:, 0:hw]], axis=1)
    gc = g2_ref[:, hw:hw + d]
    m = (jax.nn.sigmoid(ga) * _dot(ya_ref[...].astype(BF16), wa_ref[...])
         + jax.nn.sigmoid(gb) * _dot(yb_ref[...].astype(BF16), wb_ref[...])
         + jax.nn.sigmoid(gc) * _dot(yc.astype(BF16), wc_ref[...]))
    y = _dot(m.astype(BF16), wo_ref[...])
    z = alpha * x_ref[...] + gt_ref[...] * y
    out_ref[...] = _layer_norm(z, lg_ref[...], lb_ref[...])


def _merge(x, mod, ya, yb, yc_parts, proj, g_off, wa, wb, wc, wo, ln_g, ln_b, *, tm, alpha):
    m, d = x.shape
    groups, rpg, _ = mod.shape
    tpg = (m // groups) // tm
    row = lambda w, c=0: pl.BlockSpec((tm, w), lambda i, c=c: (i, c))
    full = lambda a: pl.BlockSpec(a.shape, lambda i: (0, 0))
    gw = 3 * d // 2
    in_specs = [row(d), pl.BlockSpec((None, rpg, d), lambda i: (i // tpg, 0, 5)), row(ya.shape[1]), row(yb.shape[1])]
    args = [x, mod, ya, yb]
    if isinstance(yc_parts, tuple):
        outs, lses = yc_parts
        n_lse = len(outs)
        in_specs += [row(o.shape[1]) for o in outs] + [row(l.shape[1]) for l in lses]
        args += list(outs) + list(lses)
    else:
        n_lse = 0
        in_specs.append(row(yc_parts.shape[1]))
        args.append(yc_parts)
    in_specs += [row(gw, g_off // gw), row(gw, g_off // gw + 1), full(wa), full(wb), full(wc), full(wo),
                 pl.BlockSpec((1, d), lambda i: (0, 0)), pl.BlockSpec((1, d), lambda i: (0, 0))]
    args += [proj, proj, wa, wb, wc, wo, ln_g.reshape(1, d), ln_b.reshape(1, d)]
    return pl.pallas_call(
        functools.partial(_merge_kernel, alpha=alpha, n_lse=n_lse, d=d),
        grid=(m // tm,),
        in_specs=in_specs,
        out_specs=row(d),
        out_shape=jax.ShapeDtypeStruct((m, d), F32),
        compiler_params=_params("arbitrary"),
        name="merge",
    )(*args)


def _keys_on_lanes(cache):
    return jnp.transpose(cache, (0, 1, 3, 4, 2))


def _keys_on_rows(cache_t):
    return jnp.transpose(cache_t, (0, 1, 4, 2, 3))


def kernel(x_prompt, x_sample, state_conv, cache_swa_k, cache_swa_v, cache_dil0_k, cache_dil0_v,
           cache_dil1_k, cache_dil1_v, cache_dil2_k, cache_dil2_v, c_prompt, c_sample, w_ada, b_ada,
           ln_g, ln_b, ffn_w_gate, ffn_w_up, ffn_w_down, w_in, conv_w, attn_sink,
           w_br_a, w_br_b, w_br_c, w_out):
    b, s, d = x_prompt.shape
    bd = x_sample.shape[0]
    depth = w_in.shape[0]
    a_w = w_br_a.shape[1]
    swa_q = w_br_b.shape[1]
    dil_w = w_br_c.shape[1]
    swa_kv = cache_swa_k.shape[3] * HEAD_DIM
    n_kv = swa_kv // HEAD_DIM
    swa_group = swa_q // swa_kv
    n_dh = dil_w // HEAD_DIM
    dq = len(DILATIONS) * dil_w
    alpha = (2.0 * depth) ** 0.25
    off_bq = 3 * a_w
    off_bk = off_bq + swa_q
    off_bv = off_bk + swa_kv
    off_cq = off_bv + swa_kv
    off_ck = off_cq + dq
    off_cv = off_ck + dq
    off_g = off_cv + dq
    f_dim = ffn_w_gate.shape[-1]

    tm_p = min(512, s)
    tf = _largest_tile(f_dim, 512)
    tn_in = _largest_tile(w_in.shape[-1], 1024)
    tn_ada = _largest_tile(w_ada.shape[-1], 1024)
    tm_merge = min(256, s)
    tm_conv = min(512, s)

    n_c = b + bd
    pad = (-n_c) % 8
    c_all = jnp.concatenate([c_prompt, c_sample, jnp.zeros((pad, d), F32)], axis=0)
    mod_all = _ada(c_all, w_ada, b_ada, tn_ada)

    wg_b, wu_b, wd_b = ffn_w_gate.astype(BF16), ffn_w_up.astype(BF16), ffn_w_down.astype(BF16)
    w_in_b = w_in.astype(BF16)
    wa_b, wb_b, wc_b, wo_b = (w.astype(BF16) for w in (w_br_a, w_br_b, w_br_c, w_out))

    dec_groups = [(_keys_on_lanes(cache_swa_k), _keys_on_lanes(cache_swa_v), 1, 8),
                  (_keys_on_lanes(cache_dil0_k), _keys_on_lanes(cache_dil0_v), DILATIONS[0], 8),
                  (_keys_on_lanes(cache_dil1_k), _keys_on_lanes(cache_dil1_v), DILATIONS[1], 4),
                  (_keys_on_lanes(cache_dil2_k), _keys_on_lanes(cache_dil2_v), DILATIONS[2], 1)]
    dec_acc = [(None, None)] * len(dec_groups)
    conv2 = state_conv.reshape(depth, bd, 2 * a_w)

    xp = x_prompt.reshape(b * s, d)
    xs = x_sample.reshape(bd, d)
    new_p = [[] for _ in range(9)]
    conv_s = []
    for l in range(depth):
        mod_p = mod_all[l, :b].reshape(b, 1, -1)
        mod_s = mod_all[l, b:b + bd].reshape(1, bd, -1)
        ffn = lambda x, mod, sub, i, tm: _ffn(x, mod, sub, wg_b[l, i], wu_b[l, i], wd_b[l, i],
                                              ln_g[l, sub], ln_b[l, sub], tm=tm, tf=tf, alpha=alpha, res_w=0.5)
        xp = ffn(xp, mod_p, 0, 0, tm_p)
        xs = ffn(xs, mod_s, 0, 0, bd)
        pp = _proj(xp, mod_p, w_in_b[l], tm=tm_p, tn=tn_in)
        p3 = pp.reshape(b, s, -1)
        ya_p, u_tail = _conv_prompt(p3, conv_w[l], a_w, tm_conv)
        yb_p = _band_attention(p3, 1, off_bq, off_bk, off_bv, n_kv, swa_group, attn_sink[l])
        outs, lses = [], []
        for gi, r in enumerate(DILATIONS):
            og, lg = _band_attention(p3, r, off_cq + gi * dil_w, off_ck + gi * dil_w, off_cv + gi * dil_w,
                                     n_dh, 1, None)
            outs.append(og.reshape(b * s, dil_w))
            lses.append(lg.reshape(b * s, dil_w))
        xp = _merge(xp, mod_p, ya_p.reshape(b * s, a_w), yb_p.reshape(b * s, swa_q), (outs, lses), pp, off_g,
                    wa_b[l], wb_b[l], wc_b[l], wo_b[l], ln_g[l, 1], ln_b[l, 1], tm=tm_merge, alpha=alpha)
        ps = _proj(xs, mod_s, w_in_b[l], tm=bd, tn=tn_in)
        ya_s, st_s = _conv_step(ps, l, conv2, conv_w[l], a_w)
        conv_s.append(st_s.reshape(bd, CONV_WIDTH - 1, a_w))
        pst = ps[:, off_bq:off_g].T
        col = lambda off, width: pst[off - off_bq:off - off_bq + width]
        outs, lses = [], []
        for gidx, (kc, vc, dil, bb) in enumerate(dec_groups):
            if gidx == 0:
                qt, knt, vnt, sink = col(off_bq, swa_q), col(off_bk, swa_kv), col(off_bv, swa_kv), attn_sink[l]
            else:
                gi = gidx - 1
                qt, knt, vnt, sink = (col(off_cq + gi * dil_w, dil_w), col(off_ck + gi * dil_w, dil_w),
                                      col(off_cv + gi * dil_w, dil_w), None)
            ko, vo, ot, lt = _decode_group(qt, knt, vnt, sink, kc, vc, *dec_acc[gidx], l, dil, bb)
            dec_acc[gidx] = (ko, vo)
            outs.append(ot.T)
            lses.append(lt.T)
        xs = _merge(xs, mod_s, ya_s, outs[0], (outs[1:], lses[1:]), ps, off_g,
                    wa_b[l], wb_b[l], wc_b[l], wo_b[l], ln_g[l, 1], ln_b[l, 1], tm=bd, alpha=alpha)
        xp = ffn(xp, mod_p, 2, 1, tm_p)
        xs = ffn(xs, mod_s, 2, 1, bd)

        rows = min(SPAN, s)
        st_p = [u_tail[:, 8 - (CONV_WIDTH - 1):],
                p3[:, s - rows:, off_bk:off_bk + swa_kv].reshape(b, rows, n_kv, HEAD_DIM),
                p3[:, s - rows:, off_bv:off_bv + swa_kv].reshape(b, rows, n_kv, HEAD_DIM)]
        for gi, r in enumerate(DILATIONS):
            rows = min(SPAN * r, s)
            for off in (off_ck, off_cv):
                c0 = off + gi * dil_w
                st_p.append(p3[:, s - rows:, c0:c0 + dil_w].reshape(b, rows, n_dh, HEAD_DIM))
        for acc, v in zip(new_p, st_p):
            acc.append(v)

    P = [jnp.stack(a, 0) for a in new_p]
    S = [jnp.stack(conv_s, 0)]
    for ko, vo in dec_acc:
        S += [_keys_on_rows(ko), _keys_on_rows(vo)]
    out = [xp.reshape(b, s, d), xs.reshape(bd, 1, d)]
    for pa, sa in zip(P, S):
        out += [pa, sa]
    return tuple(out)
```

```python
import functools

import jax
import jax.numpy as jnp
from jax import lax
from jax.experimental import pallas as pl
from jax.experimental.pallas import tpu as pltpu

HEAD_DIM = 64
SPAN = 128
CONV_WIDTH = 3
N_SUB = 3
DILATIONS = (1, 4, 16)
LN_EPS = 1e-5
SCALE = HEAD_DIM ** -0.5
SUBLANES = 8
LANES = 128
VMEM_LIMIT_BYTES = 58 * 1024 * 1024

F32 = jnp.float32
BF16 = jnp.bfloat16


def _params(*sem):
    return pltpu.CompilerParams(dimension_semantics=sem, vmem_limit_bytes=VMEM_LIMIT_BYTES)


def _largest_tile(n, cap):
    return max(t for t in range(128, cap + 1, 128) if n % t == 0)


def _dot(a, b):
    return jnp.dot(a, b, preferred_element_type=F32)


def _dot_nt(a, b):
    return lax.dot_general(a, b, (((1,), (1,)), ((), ())), preferred_element_type=F32)


def _layer_norm(z, g, b):
    mu = jnp.mean(z, axis=-1, keepdims=True)
    zc = z - mu
    var = jnp.mean(zc * zc, axis=-1, keepdims=True)
    return zc * lax.rsqrt(var + LN_EPS) * g + b


def _silu(x):
    return x * jax.nn.sigmoid(x)


def _mod_specs(mod, layer, cols, tiles_per_group, d):
    if mod.ndim == 4:
        return [pl.BlockSpec((None, None, 1, d), lambda i, *_, c=c: (layer, i // tiles_per_group, 0, c))
                for c in cols]
    return [pl.BlockSpec((None, mod.shape[1], d), lambda i, *_, c=c: (layer, 0, c)) for c in cols]


def _ln_spec(layer, sub, d):
    return pl.BlockSpec((None, 1, d), lambda *_: (layer * N_SUB + sub, 0, 0))


def _ada_kernel(c_ref, w_ref, b_ref, os_ref, op_ref, *, bd, b):
    a = _silu(c_ref[...]).astype(BF16)
    y = _dot(a, w_ref[...].astype(BF16)) + b_ref[...]
    os_ref[...] = y[0:bd]
    for i in range(b):
        op_ref[i] = y[bd + i:bd + i + 1]


def _ada(c_rows, bd, b, w_ada, b_ada, tn):
    depth, d, n = w_ada.shape
    r = c_rows.shape[0]
    return pl.pallas_call(
        functools.partial(_ada_kernel, bd=bd, b=b),
        grid=(depth, n // tn),
        in_specs=[pl.BlockSpec((r, d), lambda l, j: (0, 0)),
                  pl.BlockSpec((None, d, tn), lambda l, j: (l, 0, j)),
                  pl.BlockSpec((None, 1, tn), lambda l, j: (l, 0, j))],
        out_specs=[pl.BlockSpec((None, bd, tn), lambda l, j: (l, 0, j)),
                   pl.BlockSpec((None, b, 1, tn), lambda l, j: (l, 0, 0, j))],
        out_shape=[jax.ShapeDtypeStruct((depth, bd, n), F32), jax.ShapeDtypeStruct((depth, b, 1, n), F32)],
        compiler_params=_params("arbitrary", "arbitrary"),
        name="adaln",
    )(c_rows, w_ada, b_ada.reshape(depth, 1, n))


def _ffn_kernel(x_ref, sh_ref, sc_ref, gt_ref, wg_ref, wu_ref, wd_ref, lg_ref, lb_ref,
                o_ref, h_ref, *, alpha, res_w):
    f = pl.program_id(1)

    @pl.when(f == 0)
    def _():
        h_ref[...] = (x_ref[...] * (1.0 + sc_ref[...]) + sh_ref[...]).astype(BF16)

    h = h_ref[...]
    g = _dot(h, wg_ref[...])
    u = _dot(h, wu_ref[...])
    y = _dot((_silu(g) * u).astype(BF16), wd_ref[...])

    @pl.when(f == 0)
    def _():
        o_ref[...] = y

    @pl.when(f > 0)
    def _():
        o_ref[...] += y

    @pl.when(f == pl.num_programs(1) - 1)
    def _():
        z = alpha * x_ref[...] + res_w * gt_ref[...] * o_ref[...]
        o_ref[...] = _layer_norm(z, lg_ref[...], lb_ref[...])


def _ffn(x, mod, layer, sub, which, wg, wu, wd, ln_g, ln_b, *, tm, tf, alpha, res_w, single_buffer_x):
    m, d = x.shape
    f_dim = wg.shape[-1]
    groups = mod.shape[1] if mod.ndim == 4 else 1
    tpg = (m // groups) // tm
    xmode = dict(pipeline_mode=pl.Buffered(1)) if single_buffer_x else {}
    return pl.pallas_call(
        functools.partial(_ffn_kernel, alpha=alpha, res_w=res_w),
        grid=(m // tm, f_dim // tf),
        in_specs=[pl.BlockSpec((tm, d), lambda i, j: (i, 0), **xmode)]
        + _mod_specs(mod, layer, (3 * sub, 3 * sub + 1, 3 * sub + 2), tpg, d)
        + [pl.BlockSpec((None, None, d, tf), lambda i, j: (layer, which, 0, j)),
           pl.BlockSpec((None, None, d, tf), lambda i, j: (layer, which, 0, j)),
           pl.BlockSpec((None, None, tf, d), lambda i, j: (layer, which, j, 0)),
           _ln_spec(layer, sub, d), _ln_spec(layer, sub, d)],
        out_specs=pl.BlockSpec((tm, d), lambda i, j: (i, 0)),
        out_shape=jax.ShapeDtypeStruct((m, d), F32),
        scratch_shapes=[pltpu.VMEM((tm, d), BF16)],
        compiler_params=_params("arbitrary", "arbitrary"),
        name="ffn",
    )(x, mod, mod, mod, wg, wu, wd, ln_g, ln_b)


def _proj_kernel(x_ref, sh_ref, sc_ref, w_ref, o_ref, h_ref):
    @pl.when(pl.program_id(1) == 0)
    def _():
        h_ref[...] = (x_ref[...] * (1.0 + sc_ref[...]) + sh_ref[...]).astype(BF16)

    o_ref[...] = _dot(h_ref[...], w_ref[...])


def _proj(x, mod, layer, w_in, *, tm, tn):
    m, d = x.shape
    n = w_in.shape[-1]
    groups = mod.shape[1] if mod.ndim == 4 else 1
    tpg = (m // groups) // tm
    return pl.pallas_call(
        _proj_kernel,
        grid=(m // tm, n // tn),
        in_specs=[pl.BlockSpec((tm, d), lambda i, j: (i, 0))]
        + _mod_specs(mod, layer, (3, 4), tpg, d)
        + [pl.BlockSpec((None, d, tn), lambda i, j: (layer, 0, j))],
        out_specs=pl.BlockSpec((tm, tn), lambda i, j: (i, j)),
        out_shape=jax.ShapeDtypeStruct((m, n), F32),
        scratch_shapes=[pltpu.VMEM((tm, d), BF16)],
        compiler_params=_params("arbitrary", "arbitrary"),
        name="proj_in",
    )(x, mod, mod, w_in)


def _band_kernel(*refs, n_heads, group, dil, has_sink, has_lse):
    q_ref, kp_ref, kc_ref, vp_ref, vc_ref = refs[:5]
    pos = 5
    sink_ref = None
    if has_sink:
        sink_ref = refs[pos]
        pos += 1
    o_ref = refs[pos]
    lse_ref = refs[pos + 1] if has_lse else None
    pos += 2 if has_lse else 1

    first_chunk = pl.program_id(1) == 0
    qi = lax.broadcasted_iota(jnp.int32, (SPAN, 2 * SPAN), 0)
    ki = lax.broadcasted_iota(jnp.int32, (SPAN, 2 * SPAN), 1)
    dist = qi + SPAN - ki
    valid = (dist >= 0) & (dist <= SPAN) & ((ki >= SPAN) | jnp.logical_not(first_chunk))
    valid_g = jnp.concatenate([valid] * group, axis=0) if group > 1 else valid

    if dil > 1:
        q_t, kp_t, kc_t, vp_t, vc_t, o_t, lse_t = refs[pos:pos + 7]
        for src, dst in ((q_ref, q_t), (kp_ref, kp_t), (kc_ref, kc_t), (vp_ref, vp_t), (vc_ref, vc_t)):
            for c in range(dst.shape[0]):
                dst[c] = src[:, c * LANES:(c + 1) * LANES]

    def read(ref, tiled, rho):
        if dil == 1:
            return ref[...]
        return jnp.concatenate([tiled[c, pl.ds(rho, SPAN, stride=dil), :] for c in range(tiled.shape[0])], axis=1)

    def write(ref, tiled, rho, val):
        if dil == 1:
            ref[...] = val
        else:
            for c in range(tiled.shape[0]):
                tiled[c, pl.ds(rho, SPAN, stride=dil), :] = val[:, c * LANES:(c + 1) * LANES]

    def residue(rho, carry):
        q = read(q_ref, q_t if dil > 1 else None, rho).astype(BF16)
        k = jnp.concatenate([read(kp_ref, kp_t if dil > 1 else None, rho),
                             read(kc_ref, kc_t if dil > 1 else None, rho)], axis=0).astype(BF16)
        v = jnp.concatenate([read(vp_ref, vp_t if dil > 1 else None, rho),
                             read(vc_ref, vc_t if dil > 1 else None, rho)], axis=0).astype(BF16)
        outs, lses = [], []
        for h in range(n_heads // group):
            hqs = range(h * group, (h + 1) * group)
            kh = k[:, h * HEAD_DIM:(h + 1) * HEAD_DIM]
            vh = v[:, h * HEAD_DIM:(h + 1) * HEAD_DIM]
            qg = jnp.concatenate([q[:, hq * HEAD_DIM:(hq + 1) * HEAD_DIM] for hq in hqs], axis=0)
            s = _dot_nt(qg, kh) * SCALE
            s = jnp.where(valid_g, s, -jnp.inf)
            m = jnp.max(s, axis=-1, keepdims=True)
            if has_sink:
                sk = jnp.concatenate([jnp.broadcast_to(sink_ref[:, hq:hq + 1], (SPAN, 1)) for hq in hqs], axis=0)
                m = jnp.maximum(m, sk)
            e = jnp.exp(s - m)
            den = jnp.sum(e, axis=-1, keepdims=True)
            if has_sink:
                den = den + jnp.exp(sk - m)
            og = _dot(e.astype(BF16), vh) / den
            lg = m + jnp.log(den)
            for g in range(group):
                outs.append(og[g * SPAN:(g + 1) * SPAN])
                if has_lse:
                    lses.append(jnp.broadcast_to(lg[g * SPAN:(g + 1) * SPAN], (SPAN, HEAD_DIM)))
        write(o_ref, o_t if dil > 1 else None, rho, jnp.concatenate(outs, axis=1))
        if has_lse:
            write(lse_ref, lse_t if dil > 1 else None, rho, jnp.concatenate(lses, axis=1))
        return carry

    if dil > 1:
        lax.fori_loop(0, dil, residue, 0)
        for c in range(o_t.shape[0]):
            o_ref[:, c * LANES:(c + 1) * LANES] = o_t[c]
            if has_lse:
                lse_ref[:, c * LANES:(c + 1) * LANES] = lse_t[c]
    else:
        residue(0, 0)


def _band_attention(p3, dil, q_off, k_off, v_off, n_kv, group, sink, heads_per_step):
    b, s, n = p3.shape
    chunk = SPAN * dil
    nsteps = n_kv // heads_per_step
    qw, kw = heads_per_step * group * HEAD_DIM, heads_per_step * HEAD_DIM
    qspec = pl.BlockSpec((None, chunk, qw), lambda bi, c, hb: (bi, c, q_off // qw + hb))

    def kvspec(off, prev):
        if prev:
            return pl.BlockSpec((None, chunk, kw), lambda bi, c, hb: (bi, jnp.maximum(c - 1, 0), off // kw + hb))
        return pl.BlockSpec((None, chunk, kw), lambda bi, c, hb: (bi, c, off // kw + hb))

    in_specs = [qspec, kvspec(k_off, True), kvspec(k_off, False), kvspec(v_off, True), kvspec(v_off, False)]
    args = [p3] * 5
    has_sink = sink is not None
    if has_sink:
        in_specs.append(pl.BlockSpec((None, 1, qw // HEAD_DIM), lambda bi, c, hb: (hb, 0, 0)))
        args.append(sink.reshape(nsteps, 1, qw // HEAD_DIM))
    ospec = pl.BlockSpec((None, chunk, qw), lambda bi, c, hb: (bi, c, hb))
    oshape = jax.ShapeDtypeStruct((b, s, n_kv * group * HEAD_DIM), F32)
    has_lse = not has_sink
    scratch = []
    if dil > 1:
        tiled = lambda width: pltpu.VMEM((width // LANES, chunk, LANES), F32)
        scratch = [tiled(qw)] + [tiled(kw)] * 4 + [tiled(qw)] * 2
    return pl.pallas_call(
        functools.partial(_band_kernel, n_heads=heads_per_step * group, group=group, dil=dil,
                          has_sink=has_sink, has_lse=has_lse),
        grid=(b, s // chunk, nsteps),
        in_specs=in_specs,
        out_specs=[ospec, ospec] if has_lse else ospec,
        out_shape=[oshape, oshape] if has_lse else oshape,
        scratch_shapes=scratch,
        compiler_params=_params("arbitrary", "arbitrary", "arbitrary"),
        name=f"band_attn_r{dil}",
    )(*args)


def _conv_kernel(ax_ref, ab_ref, ac_ref, axp_ref, acp_ref, w_ref, y_ref, tail_ref, ue_ref, *, tm):
    i = pl.program_id(1)
    u = ac_ref[...] * ax_ref[...]
    halo = acp_ref[...] * axp_ref[...]
    ue_ref[0:SUBLANES, :] = jnp.where(i > 0, halo, 0.0)
    ue_ref[SUBLANES:, :] = u
    u1 = ue_ref[pl.ds(SUBLANES - 1, tm), :]
    u2 = ue_ref[pl.ds(SUBLANES - 2, tm), :]
    y_ref[...] = ab_ref[...] * (w_ref[0:1, :] * u2 + w_ref[1:2, :] * u1 + w_ref[2:3, :] * u)
    tail_ref[...] = u[tm - SUBLANES:, :]


def _conv_prompt(p3, layer, conv_w, a_w, tm):
    b, s, n = p3.shape
    cur = lambda c: pl.BlockSpec((None, tm, a_w), lambda bi, i, c=c: (bi, i, c))
    prev = lambda c: pl.BlockSpec((None, SUBLANES, a_w),
                                  lambda bi, i, c=c: (bi, jnp.maximum(i * (tm // SUBLANES) - 1, 0), c))
    return pl.pallas_call(
        functools.partial(_conv_kernel, tm=tm),
        grid=(b, s // tm),
        in_specs=[cur(0), cur(1), cur(2), prev(0), prev(2),
                  pl.BlockSpec((None, CONV_WIDTH, a_w), lambda bi, i: (layer, 0, 0))],
        out_specs=[pl.BlockSpec((None, tm, a_w), lambda bi, i: (bi, i, 0)),
                   pl.BlockSpec((None, SUBLANES, a_w), lambda bi, i: (bi, 0, 0))],
        out_shape=[jax.ShapeDtypeStruct((b, s, a_w), F32), jax.ShapeDtypeStruct((b, SUBLANES, a_w), F32)],
        scratch_shapes=[pltpu.VMEM((tm + SUBLANES, a_w), F32)],
        compiler_params=_params("arbitrary", "arbitrary"),
        name="short_conv",
    )(p3, p3, p3, p3, p3, conv_w)


def _conv_step_kernel(ax_ref, ab_ref, ac_ref, st_ref, cw_ref, ya_ref, st_out_ref):
    a_w = ax_ref.shape[1]
    u = ac_ref[...] * ax_ref[...]
    s0 = st_ref[:, 0:a_w]
    s1 = st_ref[:, a_w:2 * a_w]
    ya_ref[...] = ab_ref[...] * (cw_ref[0:1, :] * s0 + cw_ref[1:2, :] * s1 + cw_ref[2:3, :] * u)
    st_out_ref[:, 0:a_w] = s1
    st_out_ref[:, a_w:2 * a_w] = u


def _conv_step(ps, layer, conv_state2, conv_w, a_w):
    bd = ps.shape[0]
    col = lambda c: pl.BlockSpec((bd, a_w), lambda i, c=c: (0, c))
    return pl.pallas_call(
        _conv_step_kernel,
        grid=(1,),
        in_specs=[col(0), col(1), col(2),
                  pl.BlockSpec((None, bd, 2 * a_w), lambda i: (layer, 0, 0)),
                  pl.BlockSpec((None, CONV_WIDTH, a_w), lambda i: (layer, 0, 0))],
        out_specs=[pl.BlockSpec((bd, a_w), lambda i: (0, 0)), pl.BlockSpec((bd, 2 * a_w), lambda i: (0, 0))],
        out_shape=[jax.ShapeDtypeStruct((bd, a_w), F32), jax.ShapeDtypeStruct((bd, 2 * a_w), F32)],
        compiler_params=_params("arbitrary"),
        name="conv_step",
    )(ps, ps, ps, conv_state2, conv_w)


def _pick_column(x, onehot):
    return jnp.sum(jnp.where(onehot, x, 0.0), axis=1, keepdims=True)


def _dec_kernel(*refs, bb, n_kv, group, dil, has_sink, aliased):
    qt_ref, knt_ref, vnt_ref = refs[:3]
    pos = 3
    sink_ref = None
    if has_sink:
        sink_ref = refs[pos]
        pos += 1
    k_ref, v_ref = refs[pos:pos + 2]
    pos += 4 if aliased else 2
    ko_ref, vo_ref, ot_ref, lt_ref = refs[pos:pos + 4]

    step = pl.program_id(0)
    w = k_ref.shape[-1]
    bd = qt_ref.shape[1]
    n_q = n_kv * group

    @pl.when(step == 0)
    def _():
        ot_ref[...] = jnp.zeros_like(ot_ref)
        lt_ref[...] = jnp.zeros_like(lt_ref)

    lane_b = lax.broadcasted_iota(jnp.int32, (1, bd), 1)
    row_b = lax.broadcasted_iota(jnp.int32, (bd, LANES), 0)
    lane_w = lax.broadcasted_iota(jnp.int32, (1, w), 1)
    key_ok = (lane_w % dil) == 0
    last = lane_w == w - 1
    head = lambda x, i: x[i * HEAD_DIM:(i + 1) * HEAD_DIM]

    def body(i, carry):
        onehot = lane_b == step * bb + i
        q_all = _dot(qt_ref[...].astype(BF16), (row_b == step * bb + i).astype(BF16))
        kn_all = _pick_column(knt_ref[...], onehot)
        vn_all = _pick_column(vnt_ref[...], onehot)
        s_rows, sn_rows = [], []
        for h in range(n_kv):
            k = k_ref[i, h]
            kn = head(kn_all, h)
            ko_ref[i, h] = jnp.where(last, kn, pltpu.roll(k, w - 1, 1))
            for g in range(group):
                q = head(q_all, h * group + g)
                q_wide = jnp.tile(q, (1, w // LANES)) if w > LANES else q
                s_rows.append(jnp.sum(k * q_wide, axis=0, keepdims=True))
                sn_rows.append(jnp.sum(kn * q[:, 0:1], axis=0, keepdims=True))
        s = jnp.where(key_ok, jnp.concatenate(s_rows, axis=0) * SCALE, -jnp.inf)
        s_new = jnp.concatenate(sn_rows, axis=0) * SCALE
        m = jnp.maximum(jnp.max(s, axis=1, keepdims=True), s_new)
        if has_sink:
            m = jnp.maximum(m, sink_ref[...])
        e = jnp.exp(s - m)
        e_new = jnp.exp(s_new - m)
        den = jnp.sum(e, axis=1, keepdims=True) + e_new
        if has_sink:
            den = den + jnp.exp(sink_ref[...] - m)
        p = e / den
        p_new = e_new / den
        o_cols = []
        for h in range(n_kv):
            v = v_ref[i, h]
            vn = head(vn_all, h)
            vo_ref[i, h] = jnp.where(last, vn, pltpu.roll(v, w - 1, 1))
            for g in range(group):
                hq = h * group + g
                o_cols.append(jnp.sum(v * p[hq:hq + 1, :], axis=1, keepdims=True) + p_new[hq:hq + 1, :] * vn)
        ot_ref[...] = jnp.where(onehot, jnp.concatenate(o_cols, axis=0), ot_ref[...])
        lt_ref[...] = jnp.where(onehot, m + jnp.log(den), lt_ref[...])
        return carry

    lax.fori_loop(0, bb, body, 0)
    del n_q


def _decode_group(qt, knt, vnt, sink, k_cache, v_cache, k_acc, v_acc, layer, dil, bb):
    depth, bd, n_kv, _, w = k_cache.shape
    hq = qt.shape[0] // HEAD_DIM
    full = lambda shape: pl.BlockSpec(shape, lambda i: (0, 0))
    cache = pl.BlockSpec((None, bb, n_kv, HEAD_DIM, w), lambda i: (layer, i, 0, 0, 0))
    in_specs = [full(qt.shape), full(knt.shape), full(vnt.shape)]
    args = [qt, knt, vnt]
    if sink is not None:
        in_specs.append(pl.BlockSpec((None, hq, 1), lambda i: (layer, 0, 0)))
        args.append(sink)
    in_specs += [cache, cache]
    args += [k_cache, v_cache]
    aliases = {}
    if k_acc is not None:
        aliases = {len(args): 0, len(args) + 1: 1}
        in_specs += [pl.BlockSpec(memory_space=pl.ANY), pl.BlockSpec(memory_space=pl.ANY)]
        args += [k_acc, v_acc]
    return pl.pallas_call(
        functools.partial(_dec_kernel, bb=bb, n_kv=n_kv, group=hq // n_kv, dil=dil,
                          has_sink=sink is not None, aliased=k_acc is not None),
        grid=(bd // bb,),
        in_specs=in_specs,
        out_specs=[cache, cache, full(qt.shape), full((hq, bd))],
        out_shape=[jax.ShapeDtypeStruct(k_cache.shape, F32), jax.ShapeDtypeStruct(v_cache.shape, F32),
                   jax.ShapeDtypeStruct(qt.shape, F32), jax.ShapeDtypeStruct((hq, bd), F32)],
        input_output_aliases=aliases,
        compiler_params=_params("arbitrary"),
        name=f"decode_w{w}",
    )(*args)


def _merge_kernel(*refs, alpha, n_lse, d):
    x_ref, gt_ref, ya_ref, yb_ref = refs[:4]
    pos = 4
    o_refs = refs[pos:pos + n_lse]
    l_refs = refs[pos + n_lse:pos + 2 * n_lse]
    pos += 2 * n_lse
    ls = [r[...] for r in l_refs]
    mx = functools.reduce(jnp.maximum, ls)
    ws = [jnp.exp(l - mx) for l in ls]
    tot = functools.reduce(lambda a, b: a + b, ws)
    yc = functools.reduce(lambda a, b: a + b, [(w / tot) * o[...] for w, o in zip(ws, o_refs)])
    g1_ref, g2_ref, wa_ref, wb_ref, wc_ref, wo_ref, lg_ref, lb_ref, out_ref = refs[pos:]
    hw = d // 2
    ga = g1_ref[:, 0:d]
    gb = jnp.concatenate([g1_ref[:, d:d + hw], g2_ref[:, 0:hw]], axis=1)
    gc = g2_ref[:, hw:hw + d]
    m = (jax.nn.sigmoid(ga) * _dot(ya_ref[...].astype(BF16), wa_ref[...])
         + jax.nn.sigmoid(gb) * _dot(yb_ref[...].astype(BF16), wb_ref[...])
         + jax.nn.sigmoid(gc) * _dot(yc.astype(BF16), wc_ref[...]))
    y = _dot(m.astype(BF16), wo_ref[...])
    z = alpha * x_ref[...] + gt_ref[...] * y
    out_ref[...] = _layer_norm(z, lg_ref[...], lb_ref[...])


def _merge(x, mod, layer, ya, yb, outs, lses, proj, g_off, wa, wb, wc, wo, ln_g, ln_b, *, tm, alpha):
    m, d = x.shape
    groups = mod.shape[1] if mod.ndim == 4 else 1
    tpg = (m // groups) // tm
    row = lambda w, c=0: pl.BlockSpec((tm, w), lambda i, c=c: (i, c))
    stacked = lambda a: pl.BlockSpec((None,) + a.shape[1:], lambda i: (layer, 0, 0))
    gw = 3 * d // 2
    in_specs = ([row(d)] + _mod_specs(mod, layer, (5,), tpg, d) + [row(ya.shape[1]), row(yb.shape[1])]
                + [row(o.shape[1]) for o in outs] + [row(l.shape[1]) for l in lses]
                + [row(gw, g_off // gw), row(gw, g_off // gw + 1), stacked(wa), stacked(wb), stacked(wc), stacked(wo),
                   _ln_spec(layer, 1, d), _ln_spec(layer, 1, d)])
    args = [x, mod, ya, yb] + list(outs) + list(lses) + [proj, proj, wa, wb, wc, wo, ln_g, ln_b]
    return pl.pallas_call(
        functools.partial(_merge_kernel, alpha=alpha, n_lse=len(outs), d=d),
        grid=(m // tm,),
        in_specs=in_specs,
        out_specs=row(d),
        out_shape=jax.ShapeDtypeStruct((m, d), F32),
        compiler_params=_params("arbitrary"),
        name="merge",
    )(*args)


def _keys_on_lanes(cache):
    return jnp.transpose(cache, (0, 1, 3, 4, 2))


def _keys_on_rows(cache_t):
    return jnp.transpose(cache_t, (0, 1, 4, 2, 3))


def kernel(x_prompt, x_sample, state_conv, cache_swa_k, cache_swa_v, cache_dil0_k, cache_dil0_v,
           cache_dil1_k, cache_dil1_v, cache_dil2_k, cache_dil2_v, c_prompt, c_sample, w_ada, b_ada,
           ln_g, ln_b, ffn_w_gate, ffn_w_up, ffn_w_down, w_in, conv_w, attn_sink,
           w_br_a, w_br_b, w_br_c, w_out):
    b, s, d = x_prompt.shape
    bd = x_sample.shape[0]
    depth = w_in.shape[0]
    a_w = w_br_a.shape[1]
    swa_q = w_br_b.shape[1]
    dil_w = w_br_c.shape[1]
    swa_kv = cache_swa_k.shape[3] * HEAD_DIM
    n_kv = swa_kv // HEAD_DIM
    swa_group = swa_q // swa_kv
    n_dh = dil_w // HEAD_DIM
    dq = len(DILATIONS) * dil_w
    alpha = (2.0 * depth) ** 0.25
    off_bq = 3 * a_w
    off_bk = off_bq + swa_q
    off_bv = off_bk + swa_kv
    off_cq = off_bv + swa_kv
    off_ck = off_cq + dq
    off_cv = off_ck + dq
    off_g = off_cv + dq
    f_dim = ffn_w_gate.shape[-1]

    tm_ffn = min(1024, s)
    tm_proj = min(512, s)
    tf = _largest_tile(f_dim, 512)
    tn_in = _largest_tile(w_in.shape[-1], 1024)
    tn_ada = _largest_tile(w_ada.shape[-1], 1024)
    tm_merge = min(256, s)
    tm_conv = min(512, s)
    dil_heads_per_step = tuple(max(LANES // HEAD_DIM, n_dh // max(1, r // 8)) for r in DILATIONS)

    pad = (-(b + bd)) % SUBLANES
    c_rows = jnp.concatenate([c_sample, c_prompt, jnp.zeros((pad, d), F32)], axis=0)
    mod_s, mod_p = _ada(c_rows, bd, b, w_ada, b_ada, tn_ada)

    wg_b, wu_b, wd_b = ffn_w_gate.astype(BF16), ffn_w_up.astype(BF16), ffn_w_down.astype(BF16)
    w_in_b = w_in.astype(BF16)
    wa_b, wb_b, wc_b, wo_b = (w.astype(BF16) for w in (w_br_a, w_br_b, w_br_c, w_out))
    ln_g3 = ln_g.reshape(depth * N_SUB, 1, d)
    ln_b3 = ln_b.reshape(depth * N_SUB, 1, d)
    sink3 = attn_sink.reshape(depth, -1, 1)

    dec_groups = [(_keys_on_lanes(cache_swa_k), _keys_on_lanes(cache_swa_v), 1, 8),
                  (_keys_on_lanes(cache_dil0_k), _keys_on_lanes(cache_dil0_v), DILATIONS[0], 8),
                  (_keys_on_lanes(cache_dil1_k), _keys_on_lanes(cache_dil1_v), DILATIONS[1], 4),
                  (_keys_on_lanes(cache_dil2_k), _keys_on_lanes(cache_dil2_v), DILATIONS[2], 1)]
    dec_acc = [(None, None)] * len(dec_groups)
    conv2 = state_conv.reshape(depth, bd, 2 * a_w)

    xp = x_prompt.reshape(b * s, d)
    xs = x_sample.reshape(bd, d)
    new_p = [[] for _ in range(9)]
    conv_s = []
    for l in range(depth):
        ffn = lambda x, mod, sub, which, tm, sb: _ffn(x, mod, l, sub, which, wg_b, wu_b, wd_b, ln_g3, ln_b3,
                                                      tm=tm, tf=tf, alpha=alpha, res_w=0.5, single_buffer_x=sb)
        xp = ffn(xp, mod_p, 0, 0, tm_ffn, True)
        xs = ffn(xs, mod_s, 0, 0, bd, False)
        pp = _proj(xp, mod_p, l, w_in_b, tm=tm_proj, tn=tn_in)
        p3 = pp.reshape(b, s, -1)
        ya_p, u_tail = _conv_prompt(p3, l, conv_w, a_w, tm_conv)
        yb_p = _band_attention(p3, 1, off_bq, off_bk, off_bv, n_kv, swa_group, attn_sink[l], n_kv)
        outs, lses = [], []
        for gi, r in enumerate(DILATIONS):
            og, lg = _band_attention(p3, r, off_cq + gi * dil_w, off_ck + gi * dil_w, off_cv + gi * dil_w,
                                     n_dh, 1, None, dil_heads_per_step[gi])
            outs.append(og.reshape(b * s, dil_w))
            lses.append(lg.reshape(b * s, dil_w))
        xp = _merge(xp, mod_p, l, ya_p.reshape(b * s, a_w), yb_p.reshape(b * s, swa_q), outs, lses, pp, off_g,
                    wa_b, wb_b, wc_b, wo_b, ln_g3, ln_b3, tm=tm_merge, alpha=alpha)
        ps = _proj(xs, mod_s, l, w_in_b, tm=bd, tn=tn_in)
        ya_s, st_s = _conv_step(ps, l, conv2, conv_w, a_w)
        conv_s.append(st_s.reshape(bd, CONV_WIDTH - 1, a_w))
        pst = ps[:, off_bq:off_g].T
        col = lambda off, width: pst[off - off_bq:off - off_bq + width]
        outs, lses = [], []
        for gidx, (kc, vc, dil, bb) in enumerate(dec_groups):
            if gidx == 0:
                qt, knt, vnt, sink = col(off_bq, swa_q), col(off_bk, swa_kv), col(off_bv, swa_kv), sink3
            else:
                gi = gidx - 1
                qt, knt, vnt, sink = (col(off_cq + gi * dil_w, dil_w), col(off_ck + gi * dil_w, dil_w),
                                      col(off_cv + gi * dil_w, dil_w), None)
            ko, vo, ot, lt = _decode_group(qt, knt, vnt, sink, kc, vc, *dec_acc[gidx], l, dil, bb)
            dec_acc[gidx] = (ko, vo)
            outs.append(ot.T)
            lses.append(jnp.repeat(lt.T, HEAD_DIM, axis=1))
        xs = _merge(xs, mod_s, l, ya_s, outs[0], outs[1:], lses[1:], ps, off_g,
                    wa_b, wb_b, wc_b, wo_b, ln_g3, ln_b3, tm=bd, alpha=alpha)
        xp = ffn(xp, mod_p, 2, 1, tm_ffn, True)
        xs = ffn(xs, mod_s, 2, 1, bd, False)

        rows = min(SPAN, s)
        st_p = [u_tail[:, SUBLANES - (CONV_WIDTH - 1):],
                p3[:, s - rows:, off_bk:off_bk + swa_kv].reshape(b, rows, n_kv, HEAD_DIM),
                p3[:, s - rows:, off_bv:off_bv + swa_kv].reshape(b, rows, n_kv, HEAD_DIM)]
        for gi, r in enumerate(DILATIONS):
            rows = min(SPAN * r, s)
            for off in (off_ck, off_cv):
                c0 = off + gi * dil_w
                st_p.append(p3[:, s - rows:, c0:c0 + dil_w].reshape(b, rows, n_dh, HEAD_DIM))
        for acc, v in zip(new_p, st_p):
            acc.append(v)

    P = [jnp.stack(a, 0) for a in new_p]
    S = [jnp.stack(conv_s, 0)]
    for ko, vo in dec_acc:
        S += [_keys_on_rows(ko), _keys_on_rows(vo)]
    out = [xp.reshape(b, s, d), xs.reshape(bd, 1, d)]
    for pa, sa in zip(P, S):
        out += [pa, sa]
    return tuple(out)
```

```python
import functools
from typing import NamedTuple

import jax
import jax.numpy as jnp
from jax import lax
from jax.experimental import pallas as pl
from jax.experimental.pallas import tpu as pltpu

HEAD_DIM = 64
SPAN = 128
CONV_WIDTH = 3
N_SUB = 3
DILATIONS = (1, 4, 16)
LN_EPS = 1e-5
SCALE = HEAD_DIM ** -0.5
SUBLANES = 8
LANES = 128
VMEM_LIMIT_BYTES = 58 * 1024 * 1024

F32 = jnp.float32
BF16 = jnp.bfloat16


def _params(*sem):
    return pltpu.CompilerParams(dimension_semantics=sem, vmem_limit_bytes=VMEM_LIMIT_BYTES)


def _largest_tile(n, cap):
    return max(t for t in range(128, cap + 1, 128) if n % t == 0)


def _dot(a, b):
    return jnp.dot(a, b, preferred_element_type=F32)


def _dot_nt(a, b):
    return lax.dot_general(a, b, (((1,), (1,)), ((), ())), preferred_element_type=F32)


def _layer_norm(z, g, b):
    mu = jnp.mean(z, axis=-1, keepdims=True)
    zc = z - mu
    var = jnp.mean(zc * zc, axis=-1, keepdims=True)
    return zc * lax.rsqrt(var + LN_EPS) * g + b


def _silu(x):
    return x * jax.nn.sigmoid(x)


def _mod_specs(mod, layer, cols, tiles_per_group, d):
    if mod.ndim == 4:
        return [pl.BlockSpec((None, None, 1, d), lambda i, *_, c=c: (layer, i // tiles_per_group, 0, c))
                for c in cols]
    return [pl.BlockSpec((None, mod.shape[1], d), lambda i, *_, c=c: (layer, 0, c)) for c in cols]


def _ln_spec(layer, sub, d):
    return pl.BlockSpec((None, 1, d), lambda *_: (layer * N_SUB + sub, 0, 0))


def _ada_kernel(c_ref, w_ref, b_ref, os_ref, op_ref, *, bd, b):
    a = _silu(c_ref[...]).astype(BF16)
    y = _dot(a, w_ref[...].astype(BF16)) + b_ref[...]
    os_ref[...] = y[0:bd]
    for i in range(b):
        op_ref[i] = y[bd + i:bd + i + 1]


def _ada(c_rows, bd, b, w_ada, b_ada, tn):
    depth, d, n = w_ada.shape
    r = c_rows.shape[0]
    return pl.pallas_call(
        functools.partial(_ada_kernel, bd=bd, b=b),
        grid=(depth, n // tn),
        in_specs=[pl.BlockSpec((r, d), lambda l, j: (0, 0)),
                  pl.BlockSpec((None, d, tn), lambda l, j: (l, 0, j)),
                  pl.BlockSpec((None, 1, tn), lambda l, j: (l, 0, j))],
        out_specs=[pl.BlockSpec((None, bd, tn), lambda l, j: (l, 0, j)),
                   pl.BlockSpec((None, b, 1, tn), lambda l, j: (l, 0, 0, j))],
        out_shape=[jax.ShapeDtypeStruct((depth, bd, n), F32), jax.ShapeDtypeStruct((depth, b, 1, n), F32)],
        compiler_params=_params("arbitrary", "arbitrary"),
        name="adaln",
    )(c_rows, w_ada, b_ada.reshape(depth, 1, n))


class _DecodeHost(NamedTuple):
    qt: jax.Array
    knt: jax.Array
    vnt: jax.Array
    k_view: jax.Array
    v_view: jax.Array
    k_acc: jax.Array | None
    v_acc: jax.Array | None
    layer: int
    first_elem: int
    n_elems: int
    dil: int


class _HostStatic(NamedTuple):
    parts: int
    first_elem: int
    n_elems: int
    dil: int
    aliased: bool


def _ffn_kernel(*refs, alpha, res_w, host):
    x_ref, sh_ref, sc_ref, gt_ref, wg_ref, wu_ref, wd_ref, lg_ref, lb_ref = refs[:9]
    pos = 9
    if host is not None:
        qt_ref, knt_ref, vnt_ref, k_ref, v_ref = refs[pos:pos + 5]
        pos += 7 if host.aliased else 5
    o_ref = refs[pos]
    pos += 1
    if host is not None:
        ko_ref, vo_ref, ot_ref, lt_ref = refs[pos:pos + 4]
        pos += 4
    h_ref = refs[pos]
    i = pl.program_id(0)
    f = pl.program_id(1)

    @pl.when(f == 0)
    def _():
        h_ref[...] = (x_ref[...] * (1.0 + sc_ref[...]) + sh_ref[...]).astype(BF16)
        o_ref[...] = jnp.zeros_like(o_ref)

    if host is not None:
        @pl.when((i == 0) & (f == 0))
        def _():
            ot_ref[...] = jnp.zeros_like(ot_ref)
            lt_ref[...] = jnp.zeros_like(lt_ref)

        unit = jnp.minimum(i * pl.num_programs(1) + f, host.n_elems * host.parts - 1)
        part = unit % host.parts
        ot, lt = _decode_unit(host.first_elem + unit // host.parts, qt_ref[part], knt_ref[part], vnt_ref[part], None,
                              k_ref, v_ref, ko_ref, vo_ref, ot_ref[part], lt_ref[part], group=1, dil=host.dil)
        ot_ref[part] = ot
        lt_ref[part] = lt

    h = h_ref[...]
    g = _dot(h, wg_ref[...])
    u = _dot(h, wu_ref[...])
    o_ref[...] += _dot((_silu(g) * u).astype(BF16), wd_ref[...])

    @pl.when(f == pl.num_programs(1) - 1)
    def _():
        z = alpha * x_ref[...] + res_w * gt_ref[...] * o_ref[...]
        o_ref[...] = _layer_norm(z, lg_ref[...], lb_ref[...])


def _ffn(x, mod, layer, sub, which, wg, wu, wd, ln_g, ln_b, *, tm, tf, alpha, res_w, host=None):
    m, d = x.shape
    f_dim = wg.shape[-1]
    groups = mod.shape[1] if mod.ndim == 4 else 1
    tpg = (m // groups) // tm
    nf = f_dim // tf
    in_specs = ([pl.BlockSpec((tm, d), lambda i, j: (i, 0))]
                + _mod_specs(mod, layer, (3 * sub, 3 * sub + 1, 3 * sub + 2), tpg, d)
                + [pl.BlockSpec((None, None, d, tf), lambda i, j: (layer, which, 0, j)),
                   pl.BlockSpec((None, None, d, tf), lambda i, j: (layer, which, 0, j)),
                   pl.BlockSpec((None, None, tf, d), lambda i, j: (layer, which, j, 0)),
                   _ln_spec(layer, sub, d), _ln_spec(layer, sub, d)])
    args = [x, mod, mod, mod, wg, wu, wd, ln_g, ln_b]
    out_specs = [pl.BlockSpec((tm, d), lambda i, j: (i, 0))]
    out_shape = [jax.ShapeDtypeStruct((m, d), F32)]
    aliases = {}
    static = None
    if host is not None:
        parts, rows, bd = host.qt.shape
        heads, _, w = host.k_view.shape[3:]
        static = _HostStatic(parts, host.first_elem, host.n_elems, host.dil, host.k_acc is not None)
        assert m // tm * nf >= host.n_elems * parts

        def unit_block(i, j):
            unit = jnp.minimum(i * nf + j, host.n_elems * parts - 1)
            return (host.layer, host.first_elem + unit // parts, unit % parts, 0, 0, 0)

        cols = pl.BlockSpec((parts, rows, bd), lambda i, j: (0, 0, 0))
        cache = pl.BlockSpec((None, None, None, heads, HEAD_DIM, w), unit_block)
        in_specs += [cols, cols, cols, cache, cache]
        args += [host.qt, host.knt, host.vnt, host.k_view, host.v_view]
        if host.k_acc is not None:
            aliases = {len(args): 1, len(args) + 1: 2}
            in_specs += [pl.BlockSpec(memory_space=pl.ANY), pl.BlockSpec(memory_space=pl.ANY)]
            args += [host.k_acc, host.v_acc]
        lse = pl.BlockSpec((parts, SUBLANES, bd), lambda i, j: (0, 0, 0))
        out_specs += [cache, cache, cols, lse]
        out_shape += [jax.ShapeDtypeStruct(host.k_view.shape, F32), jax.ShapeDtypeStruct(host.v_view.shape, F32),
                      jax.ShapeDtypeStruct((parts, rows, bd), F32), jax.ShapeDtypeStruct((parts, SUBLANES, bd), F32)]
    out = pl.pallas_call(
        functools.partial(_ffn_kernel, alpha=alpha, res_w=res_w, host=static),
        grid=(m // tm, nf),
        in_specs=in_specs,
        out_specs=out_specs,
        out_shape=out_shape,
        input_output_aliases=aliases,
        scratch_shapes=[pltpu.VMEM((tm, d), BF16)],
        compiler_params=_params("arbitrary", "arbitrary"),
        name="ffn" if host is None else "ffn_host",
    )(*args)
    return out[0] if host is None else out


def _proj_kernel(x_ref, sh_ref, sc_ref, w_ref, o_ref, h_ref):
    @pl.when(pl.program_id(1) == 0)
    def _():
        h_ref[...] = (x_ref[...] * (1.0 + sc_ref[...]) + sh_ref[...]).astype(BF16)

    o_ref[...] = _dot(h_ref[...], w_ref[...])


def _proj(x, mod, layer, w_in, *, tm, tn):
    m, d = x.shape
    n = w_in.shape[-1]
    groups = mod.shape[1] if mod.ndim == 4 else 1
    tpg = (m // groups) // tm
    return pl.pallas_call(
        _proj_kernel,
        grid=(m // tm, n // tn),
        in_specs=[pl.BlockSpec((tm, d), lambda i, j: (i, 0))]
        + _mod_specs(mod, layer, (3, 4), tpg, d)
        + [pl.BlockSpec((None, d, tn), lambda i, j: (layer, 0, j))],
        out_specs=pl.BlockSpec((tm, tn), lambda i, j: (i, j)),
        out_shape=jax.ShapeDtypeStruct((m, n), F32),
        scratch_shapes=[pltpu.VMEM((tm, d), BF16)],
        compiler_params=_params("arbitrary", "arbitrary"),
        name="proj_in",
    )(x, mod, mod, w_in)


def _band_kernel(*refs, n_heads, group, dil, has_sink, has_lse):
    q_ref, kp_ref, kc_ref, vp_ref, vc_ref = refs[:5]
    pos = 5
    sink_ref = None
    if has_sink:
        sink_ref = refs[pos]
        pos += 1
    o_ref = refs[pos]
    lse_ref = refs[pos + 1] if has_lse else None
    pos += 2 if has_lse else 1

    first_chunk = pl.program_id(1) == 0
    qi = lax.broadcasted_iota(jnp.int32, (SPAN, 2 * SPAN), 0)
    ki = lax.broadcasted_iota(jnp.int32, (SPAN, 2 * SPAN), 1)
    dist = qi + SPAN - ki
    valid = (dist >= 0) & (dist <= SPAN) & ((ki >= SPAN) | jnp.logical_not(first_chunk))
    valid_g = jnp.concatenate([valid] * group, axis=0) if group > 1 else valid

    if dil > 1:
        q_t, kp_t, kc_t, vp_t, vc_t, o_t, lse_t = refs[pos:pos + 7]
        for src, dst in ((q_ref, q_t), (kp_ref, kp_t), (kc_ref, kc_t), (vp_ref, vp_t), (vc_ref, vc_t)):
            for c in range(dst.shape[0]):
                dst[c] = src[:, c * LANES:(c + 1) * LANES]

    def read(ref, tiled, rho):
        if dil == 1:
            return ref[...]
        return jnp.concatenate([tiled[c, pl.ds(rho, SPAN, stride=dil), :] for c in range(tiled.shape[0])], axis=1)

    def write(ref, tiled, rho, val):
        if dil == 1:
            ref[...] = val
        else:
            for c in range(tiled.shape[0]):
                tiled[c, pl.ds(rho, SPAN, stride=dil), :] = val[:, c * LANES:(c + 1) * LANES]

    def residue(rho, carry):
        q = read(q_ref, q_t if dil > 1 else None, rho).astype(BF16)
        k = jnp.concatenate([read(kp_ref, kp_t if dil > 1 else None, rho),
                             read(kc_ref, kc_t if dil > 1 else None, rho)], axis=0).astype(BF16)
        v = jnp.concatenate([read(vp_ref, vp_t if dil > 1 else None, rho),
                             read(vc_ref, vc_t if dil > 1 else None, rho)], axis=0).astype(BF16)
        outs, lses = [], []
        for h in range(n_heads // group):
            hqs = range(h * group, (h + 1) * group)
            kh = k[:, h * HEAD_DIM:(h + 1) * HEAD_DIM]
            vh = v[:, h * HEAD_DIM:(h + 1) * HEAD_DIM]
            qg = jnp.concatenate([q[:, hq * HEAD_DIM:(hq + 1) * HEAD_DIM] for hq in hqs], axis=0)
            s = _dot_nt(qg, kh) * SCALE
            s = jnp.where(valid_g, s, -jnp.inf)
            m = jnp.max(s, axis=-1, keepdims=True)
            if has_sink:
                sk = jnp.concatenate([jnp.broadcast_to(sink_ref[:, hq:hq + 1], (SPAN, 1)) for hq in hqs], axis=0)
                m = jnp.maximum(m, sk)
            e = jnp.exp(s - m)
            den = jnp.sum(e, axis=-1, keepdims=True)
            if has_sink:
                den = den + jnp.exp(sk - m)
            og = _dot(e.astype(BF16), vh) / den
            lg = m + jnp.log(den)
            for g in range(group):
                outs.append(og[g * SPAN:(g + 1) * SPAN])
                if has_lse:
                    lses.append(jnp.broadcast_to(lg[g * SPAN:(g + 1) * SPAN], (SPAN, HEAD_DIM)))
        write(o_ref, o_t if dil > 1 else None, rho, jnp.concatenate(outs, axis=1))
        if has_lse:
            write(lse_ref, lse_t if dil > 1 else None, rho, jnp.concatenate(lses, axis=1))
        return carry

    if dil > 1:
        lax.fori_loop(0, dil, residue, 0)
        for c in range(o_t.shape[0]):
            o_ref[:, c * LANES:(c + 1) * LANES] = o_t[c]
            if has_lse:
                lse_ref[:, c * LANES:(c + 1) * LANES] = lse_t[c]
    else:
        residue(0, 0)


def _band_attention(p3, dil, q_off, k_off, v_off, n_kv, group, sink, heads_per_step):
    b, s, n = p3.shape
    chunk = SPAN * dil
    nsteps = n_kv // heads_per_step
    qw, kw = heads_per_step * group * HEAD_DIM, heads_per_step * HEAD_DIM
    qspec = pl.BlockSpec((None, chunk, qw), lambda bi, c, hb: (bi, c, q_off // qw + hb))

    def kvspec(off, prev):
        if prev:
            return pl.BlockSpec((None, chunk, kw), lambda bi, c, hb: (bi, jnp.maximum(c - 1, 0), off // kw + hb))
        return pl.BlockSpec((None, chunk, kw), lambda bi, c, hb: (bi, c, off // kw + hb))

    in_specs = [qspec, kvspec(k_off, True), kvspec(k_off, False), kvspec(v_off, True), kvspec(v_off, False)]
    args = [p3] * 5
    has_sink = sink is not None
    if has_sink:
        in_specs.append(pl.BlockSpec((None, 1, qw // HEAD_DIM), lambda bi, c, hb: (hb, 0, 0)))
        args.append(sink.reshape(nsteps, 1, qw // HEAD_DIM))
    ospec = pl.BlockSpec((None, chunk, qw), lambda bi, c, hb: (bi, c, hb))
    oshape = jax.ShapeDtypeStruct((b, s, n_kv * group * HEAD_DIM), F32)
    has_lse = not has_sink
    scratch = []
    if dil > 1:
        tiled = lambda width: pltpu.VMEM((width // LANES, chunk, LANES), F32)
        scratch = [tiled(qw)] + [tiled(kw)] * 4 + [tiled(qw)] * 2
    return pl.pallas_call(
        functools.partial(_band_kernel, n_heads=heads_per_step * group, group=group, dil=dil,
                          has_sink=has_sink, has_lse=has_lse),
        grid=(b, s // chunk, nsteps),
        in_specs=in_specs,
        out_specs=[ospec, ospec] if has_lse else ospec,
        out_shape=[oshape, oshape] if has_lse else oshape,
        scratch_shapes=scratch,
        compiler_params=_params("arbitrary", "arbitrary", "arbitrary"),
        name=f"band_attn_r{dil}",
    )(*args)


def _conv_kernel(ax_ref, ab_ref, ac_ref, axp_ref, acp_ref, w_ref, y_ref, tail_ref, ue_ref, *, tm):
    i = pl.program_id(1)
    u = ac_ref[...] * ax_ref[...]
    halo = acp_ref[...] * axp_ref[...]
    ue_ref[0:SUBLANES, :] = jnp.where(i > 0, halo, 0.0)
    ue_ref[SUBLANES:, :] = u
    u1 = ue_ref[pl.ds(SUBLANES - 1, tm), :]
    u2 = ue_ref[pl.ds(SUBLANES - 2, tm), :]
    y_ref[...] = ab_ref[...] * (w_ref[0:1, :] * u2 + w_ref[1:2, :] * u1 + w_ref[2:3, :] * u)
    tail_ref[...] = u[tm - SUBLANES:, :]


def _conv_prompt(p3, layer, conv_w, a_w, tm):
    b, s, n = p3.shape
    cur = lambda c: pl.BlockSpec((None, tm, a_w), lambda bi, i, c=c: (bi, i, c))
    prev = lambda c: pl.BlockSpec((None, SUBLANES, a_w),
                                  lambda bi, i, c=c: (bi, jnp.maximum(i * (tm // SUBLANES) - 1, 0), c))
    return pl.pallas_call(
        functools.partial(_conv_kernel, tm=tm),
        grid=(b, s // tm),
        in_specs=[cur(0), cur(1), cur(2), prev(0), prev(2),
                  pl.BlockSpec((None, CONV_WIDTH, a_w), lambda bi, i: (layer, 0, 0))],
        out_specs=[pl.BlockSpec((None, tm, a_w), lambda bi, i: (bi, i, 0)),
                   pl.BlockSpec((None, SUBLANES, a_w), lambda bi, i: (bi, 0, 0))],
        out_shape=[jax.ShapeDtypeStruct((b, s, a_w), F32), jax.ShapeDtypeStruct((b, SUBLANES, a_w), F32)],
        scratch_shapes=[pltpu.VMEM((tm + SUBLANES, a_w), F32)],
        compiler_params=_params("arbitrary", "arbitrary"),
        name="short_conv",
    )(p3, p3, p3, p3, p3, conv_w)


def _conv_step_kernel(ax_ref, ab_ref, ac_ref, st_ref, cw_ref, ya_ref, st_out_ref):
    a_w = ax_ref.shape[1]
    u = ac_ref[...] * ax_ref[...]
    s0 = st_ref[:, 0:a_w]
    s1 = st_ref[:, a_w:2 * a_w]
    ya_ref[...] = ab_ref[...] * (cw_ref[0:1, :] * s0 + cw_ref[1:2, :] * s1 + cw_ref[2:3, :] * u)
    st_out_ref[:, 0:a_w] = s1
    st_out_ref[:, a_w:2 * a_w] = u


def _conv_step(ps, layer, conv_state2, conv_w, a_w):
    bd = ps.shape[0]
    col = lambda c: pl.BlockSpec((bd, a_w), lambda i, c=c: (0, c))
    return pl.pallas_call(
        _conv_step_kernel,
        grid=(1,),
        in_specs=[col(0), col(1), col(2),
                  pl.BlockSpec((None, bd, 2 * a_w), lambda i: (layer, 0, 0)),
                  pl.BlockSpec((None, CONV_WIDTH, a_w), lambda i: (layer, 0, 0))],
        out_specs=[pl.BlockSpec((bd, a_w), lambda i: (0, 0)), pl.BlockSpec((bd, 2 * a_w), lambda i: (0, 0))],
        out_shape=[jax.ShapeDtypeStruct((bd, a_w), F32), jax.ShapeDtypeStruct((bd, 2 * a_w), F32)],
        compiler_params=_params("arbitrary"),
        name="conv_step",
    )(ps, ps, ps, conv_state2, conv_w)


def _pick_column(x, onehot):
    return jnp.sum(jnp.where(onehot, x, 0.0), axis=1, keepdims=True)


def _decode_unit(elem, qt, knt, vnt, sink, k_ref, v_ref, ko_ref, vo_ref, ot_old, lt_old, *, group, dil):
    n_kv, _, w = k_ref.shape
    bd = qt.shape[1]
    n_q = n_kv * group
    lane_b = lax.broadcasted_iota(jnp.int32, (1, bd), 1)
    row_b = lax.broadcasted_iota(jnp.int32, (bd, LANES), 0)
    lane_w = lax.broadcasted_iota(jnp.int32, (1, w), 1)
    key_ok = (lane_w % dil) == 0
    last = lane_w == w - 1
    head = lambda x, i: x[i * HEAD_DIM:(i + 1) * HEAD_DIM]

    onehot = lane_b == elem
    q_all = _dot(qt.astype(BF16), (row_b == elem).astype(BF16))
    kn_all = _pick_column(knt, onehot)
    vn_all = _pick_column(vnt, onehot)
    s_rows, sn_rows = [], []
    for h in range(n_kv):
        k = k_ref[h]
        kn = head(kn_all, h)
        ko_ref[h] = jnp.where(last, kn, pltpu.roll(k, w - 1, 1))
        for g in range(group):
            q = head(q_all, h * group + g)
            q_wide = jnp.tile(q, (1, w // LANES)) if w > LANES else q
            s_rows.append(jnp.sum(k * q_wide, axis=0, keepdims=True))
            sn_rows.append(jnp.sum(kn * q[:, 0:1], axis=0, keepdims=True))
    s = jnp.where(key_ok, jnp.concatenate(s_rows, axis=0) * SCALE, -jnp.inf)
    s_new = jnp.concatenate(sn_rows, axis=0) * SCALE
    m = jnp.maximum(jnp.max(s, axis=1, keepdims=True), s_new)
    if sink is not None:
        m = jnp.maximum(m, sink)
    e = jnp.exp(s - m)
    e_new = jnp.exp(s_new - m)
    den = jnp.sum(e, axis=1, keepdims=True) + e_new
    if sink is not None:
        den = den + jnp.exp(sink - m)
    p = e / den
    p_new = e_new / den
    o_cols = []
    for h in range(n_kv):
        v = v_ref[h]
        vn = head(vn_all, h)
        vo_ref[h] = jnp.where(last, vn, pltpu.roll(v, w - 1, 1))
        for g in range(group):
            hq = h * group + g
            o_cols.append(jnp.sum(v * p[hq:hq + 1, :], axis=1, keepdims=True) + p_new[hq:hq + 1, :] * vn)
    lse = m + jnp.log(den)
    if lt_old.shape[0] > n_q:
        lse = jnp.concatenate([lse, jnp.zeros((lt_old.shape[0] - n_q, 1), F32)], axis=0)
    return (jnp.where(onehot, jnp.concatenate(o_cols, axis=0), ot_old), jnp.where(onehot, lse, lt_old))


def _dec_kernel(*refs, bb, group, dil, has_sink, aliased):
    qt_ref, knt_ref, vnt_ref = refs[:3]
    pos = 3
    sink_ref = None
    if has_sink:
        sink_ref = refs[pos]
        pos += 1
    k_ref, v_ref = refs[pos:pos + 2]
    pos += 4 if aliased else 2
    ko_ref, vo_ref, ot_ref, lt_ref = refs[pos:pos + 4]
    step = pl.program_id(0)

    @pl.when(step == 0)
    def _():
        ot_ref[...] = jnp.zeros_like(ot_ref)
        lt_ref[...] = jnp.zeros_like(lt_ref)

    def body(i, carry):
        ot, lt = _decode_unit(step * bb + i, qt_ref[...], knt_ref[...], vnt_ref[...],
                              sink_ref[...] if has_sink else None,
                              k_ref.at[i], v_ref.at[i], ko_ref.at[i], vo_ref.at[i], ot_ref[...], lt_ref[...],
                              group=group, dil=dil)
        ot_ref[...] = ot
        lt_ref[...] = lt
        return carry

    lax.fori_loop(0, bb, body, 0)


def _decode_group(qt, knt, vnt, sink, k_cache, v_cache, k_acc, v_acc, layer, dil, bb):
    depth, bd, n_kv, _, w = k_cache.shape
    hq = qt.shape[0] // HEAD_DIM
    full = lambda shape: pl.BlockSpec(shape, lambda i: (0, 0))
    cache = pl.BlockSpec((None, bb, n_kv, HEAD_DIM, w), lambda i: (layer, i, 0, 0, 0))
    in_specs = [full(qt.shape), full(knt.shape), full(vnt.shape)]
    args = [qt, knt, vnt]
    if sink is not None:
        in_specs.append(pl.BlockSpec((None, hq, 1), lambda i: (layer, 0, 0)))
        args.append(sink)
    in_specs += [cache, cache]
    args += [k_cache, v_cache]
    aliases = {}
    if k_acc is not None:
        aliases = {len(args): 0, len(args) + 1: 1}
        in_specs += [pl.BlockSpec(memory_space=pl.ANY), pl.BlockSpec(memory_space=pl.ANY)]
        args += [k_acc, v_acc]
    return pl.pallas_call(
        functools.partial(_dec_kernel, bb=bb, group=hq // n_kv, dil=dil,
                          has_sink=sink is not None, aliased=k_acc is not None),
        grid=(bd // bb,),
        in_specs=in_specs,
        out_specs=[cache, cache, full(qt.shape), full((hq, bd))],
        out_shape=[jax.ShapeDtypeStruct(k_cache.shape, F32), jax.ShapeDtypeStruct(v_cache.shape, F32),
                   jax.ShapeDtypeStruct(qt.shape, F32), jax.ShapeDtypeStruct((hq, bd), F32)],
        input_output_aliases=aliases,
        compiler_params=_params("arbitrary"),
        name=f"decode_w{w}",
    )(*args)


def _merge_kernel(*refs, alpha, n_lse, d):
    x_ref, gt_ref, ya_ref, yb_ref = refs[:4]
    pos = 4
    o_refs = refs[pos:pos + n_lse]
    l_refs = refs[pos + n_lse:pos + 2 * n_lse]
    pos += 2 * n_lse
    ls = [r[...] for r in l_refs]
    mx = functools.reduce(jnp.maximum, ls)
    ws = [jnp.exp(l - mx) for l in ls]
    tot = functools.reduce(lambda a, b: a + b, ws)
    yc = functools.reduce(lambda a, b: a + b, [(w / tot) * o[...] for w, o in zip(ws, o_refs)])
    g1_ref, g2_ref, wa_ref, wb_ref, wc_ref, wo_ref, lg_ref, lb_ref, out_ref = refs[pos:]
    hw = d // 2
    ga = g1_ref[:, 0:d]
    gb = jnp.concatenate([g1_ref[:, d:d + hw], g2_ref[:, 0:hw]], axis=1)
    gc = g2_ref[:, hw:hw + d]
    m = (jax.nn.sigmoid(ga) * _dot(ya_ref[...].astype(BF16), wa_ref[...])
         + jax.nn.sigmoid(gb) * _dot(yb_ref[...].astype(BF16), wb_ref[...])
         + jax.nn.sigmoid(gc) * _dot(yc.astype(BF16), wc_ref[...]))
    y = _dot(m.astype(BF16), wo_ref[...])
    z = alpha * x_ref[...] + gt_ref[...] * y
    out_ref[...] = _layer_norm(z, lg_ref[...], lb_ref[...])


def _merge(x, mod, layer, ya, yb, outs, lses, proj, g_off, wa, wb, wc, wo, ln_g, ln_b, *, tm, alpha):
    m, d = x.shape
    groups = mod.shape[1] if mod.ndim == 4 else 1
    tpg = (m // groups) // tm
    row = lambda w, c=0: pl.BlockSpec((tm, w), lambda i, c=c: (i, c))
    stacked = lambda a: pl.BlockSpec((None,) + a.shape[1:], lambda i: (layer, 0, 0))
    gw = 3 * d // 2
    in_specs = ([row(d)] + _mod_specs(mod, layer, (5,), tpg, d) + [row(ya.shape[1]), row(yb.shape[1])]
                + [row(o.shape[1]) for o in outs] + [row(l.shape[1]) for l in lses]
                + [row(gw, g_off // gw), row(gw, g_off // gw + 1), stacked(wa), stacked(wb), stacked(wc), stacked(wo),
                   _ln_spec(layer, 1, d), _ln_spec(layer, 1, d)])
    args = [x, mod, ya, yb] + list(outs) + list(lses) + [proj, proj, wa, wb, wc, wo, ln_g, ln_b]
    return pl.pallas_call(
        functools.partial(_merge_kernel, alpha=alpha, n_lse=len(outs), d=d),
        grid=(m // tm,),
        in_specs=in_specs,
        out_specs=row(d),
        out_shape=jax.ShapeDtypeStruct((m, d), F32),
        compiler_params=_params("arbitrary"),
        name="merge",
    )(*args)


def _keys_on_lanes(cache):
    return jnp.transpose(cache, (0, 1, 3, 4, 2))


def _keys_on_rows(cache_t):
    return jnp.transpose(cache_t, (0, 1, 4, 2, 3))


def kernel(x_prompt, x_sample, state_conv, cache_swa_k, cache_swa_v, cache_dil0_k, cache_dil0_v,
           cache_dil1_k, cache_dil1_v, cache_dil2_k, cache_dil2_v, c_prompt, c_sample, w_ada, b_ada,
           ln_g, ln_b, ffn_w_gate, ffn_w_up, ffn_w_down, w_in, conv_w, attn_sink,
           w_br_a, w_br_b, w_br_c, w_out):
    b, s, d = x_prompt.shape
    bd = x_sample.shape[0]
    depth = w_in.shape[0]
    a_w = w_br_a.shape[1]
    swa_q = w_br_b.shape[1]
    dil_w = w_br_c.shape[1]
    swa_kv = cache_swa_k.shape[3] * HEAD_DIM
    n_kv = swa_kv // HEAD_DIM
    swa_group = swa_q // swa_kv
    n_dh = dil_w // HEAD_DIM
    dq = len(DILATIONS) * dil_w
    alpha = (2.0 * depth) ** 0.25
    off_bq = 3 * a_w
    off_bk = off_bq + swa_q
    off_bv = off_bk + swa_kv
    off_cq = off_bv + swa_kv
    off_ck = off_cq + dq
    off_cv = off_ck + dq
    off_g = off_cv + dq
    f_dim = ffn_w_gate.shape[-1]

    tm_ffn = min(512, s)
    tm_proj = min(512, s)
    tf = _largest_tile(f_dim, 512)
    tn_in = _largest_tile(w_in.shape[-1], 1024)
    tn_in_prompt = _largest_tile(w_in.shape[-1], 3072)
    tn_ada = _largest_tile(w_ada.shape[-1], 1024)
    tm_merge = min(256, s)
    tm_conv = min(512, s)
    dil_heads_per_step = tuple(max(LANES // HEAD_DIM, n_dh // max(1, r // 8)) for r in DILATIONS)

    pad = (-(b + bd)) % SUBLANES
    c_rows = jnp.concatenate([c_sample, c_prompt, jnp.zeros((pad, d), F32)], axis=0)
    mod_s, mod_p = _ada(c_rows, bd, b, w_ada, b_ada, tn_ada)

    wg_b, wu_b, wd_b = ffn_w_gate.astype(BF16), ffn_w_up.astype(BF16), ffn_w_down.astype(BF16)
    w_in_b = w_in.astype(BF16)
    wa_b, wb_b, wc_b, wo_b = (w.astype(BF16) for w in (w_br_a, w_br_b, w_br_c, w_out))
    ln_g3 = ln_g.reshape(depth * N_SUB, 1, d)
    ln_b3 = ln_b.reshape(depth * N_SUB, 1, d)
    sink3 = attn_sink.reshape(depth, -1, 1)

    dec_groups = [(_keys_on_lanes(cache_swa_k), _keys_on_lanes(cache_swa_v), 1, 8),
                  (_keys_on_lanes(cache_dil0_k), _keys_on_lanes(cache_dil0_v), DILATIONS[0], 8),
                  (_keys_on_lanes(cache_dil1_k), _keys_on_lanes(cache_dil1_v), DILATIONS[1], 4)]
    dec_acc = [(None, None)] * len(dec_groups)
    host_parts = 2
    host_heads = n_dh // host_parts
    host_shape = (depth, bd, host_parts, host_heads, HEAD_DIM, cache_dil2_k.shape[2])
    host_k = _keys_on_lanes(cache_dil2_k).reshape(host_shape)
    host_v = _keys_on_lanes(cache_dil2_v).reshape(host_shape)
    host_acc = (None, None)
    host_gi = len(DILATIONS) - 1
    conv2 = state_conv.reshape(depth, bd, 2 * a_w)

    xp = x_prompt.reshape(b * s, d)
    xs = x_sample.reshape(bd, d)
    new_p = [[] for _ in range(9)]
    conv_s = []
    for l in range(depth):
        ffn = lambda x, mod, sub, which, tm, host=None: _ffn(x, mod, l, sub, which, wg_b, wu_b, wd_b, ln_g3, ln_b3,
                                                             tm=tm, tf=tf, alpha=alpha, res_w=0.5, host=host)
        xs = ffn(xs, mod_s, 0, 0, bd)
        ps = _proj(xs, mod_s, l, w_in_b, tm=bd, tn=tn_in)
        ya_s, st_s = _conv_step(ps, l, conv2, conv_w, a_w)
        conv_s.append(st_s.reshape(bd, CONV_WIDTH - 1, a_w))
        pst = ps[:, off_bq:off_g].T
        col = lambda off, width: pst[off - off_bq:off - off_bq + width]
        outs_s, lses_s = [], []
        for gidx, (kc, vc, dil, bb) in enumerate(dec_groups):
            if gidx == 0:
                qt, knt, vnt, sink = col(off_bq, swa_q), col(off_bk, swa_kv), col(off_bv, swa_kv), sink3
            else:
                gi = gidx - 1
                qt, knt, vnt, sink = (col(off_cq + gi * dil_w, dil_w), col(off_ck + gi * dil_w, dil_w),
                                      col(off_cv + gi * dil_w, dil_w), None)
            ko, vo, ot, lt = _decode_group(qt, knt, vnt, sink, kc, vc, *dec_acc[gidx], l, dil, bb)
            dec_acc[gidx] = (ko, vo)
            outs_s.append(ot.T)
            lses_s.append(jnp.repeat(lt.T, HEAD_DIM, axis=1))
        host_cols = [col(off + host_gi * dil_w, dil_w).reshape(host_parts, host_heads * HEAD_DIM, bd)
                     for off in (off_cq, off_ck, off_cv)]
        make_host = lambda first: _DecodeHost(*host_cols, host_k, host_v, *host_acc, l, first, bd // 2,
                                              DILATIONS[host_gi])
        xp, hk, hv, ot_a, lt_a = ffn(xp, mod_p, 0, 0, tm_ffn, make_host(0))
        host_acc = (hk, hv)
        pp = _proj(xp, mod_p, l, w_in_b, tm=tm_proj, tn=tn_in_prompt)
        p3 = pp.reshape(b, s, -1)
        ya_p, u_tail = _conv_prompt(p3, l, conv_w, a_w, tm_conv)
        yb_p = _band_attention(p3, 1, off_bq, off_bk, off_bv, n_kv, swa_group, attn_sink[l], n_kv)
        outs, lses = [], []
        for gi, r in enumerate(DILATIONS):
            og, lg = _band_attention(p3, r, off_cq + gi * dil_w, off_ck + gi * dil_w, off_cv + gi * dil_w,
                                     n_dh, 1, None, dil_heads_per_step[gi])
            outs.append(og.reshape(b * s, dil_w))
            lses.append(lg.reshape(b * s, dil_w))
        xp = _merge(xp, mod_p, l, ya_p.reshape(b * s, a_w), yb_p.reshape(b * s, swa_q), outs, lses, pp, off_g,
                    wa_b, wb_b, wc_b, wo_b, ln_g3, ln_b3, tm=tm_merge, alpha=alpha)
        xp, hk, hv, ot_b, lt_b = ffn(xp, mod_p, 2, 1, tm_ffn, make_host(bd // 2))
        host_acc = (hk, hv)
        outs_s.append((ot_a + ot_b).reshape(dil_w, bd).T)
        lt_h = (lt_a + lt_b)[:, :host_heads].reshape(n_dh, bd)
        lses_s.append(jnp.repeat(lt_h.T, HEAD_DIM, axis=1))
        xs = _merge(xs, mod_s, l, ya_s, outs_s[0], outs_s[1:], lses_s[1:], ps, off_g,
                    wa_b, wb_b, wc_b, wo_b, ln_g3, ln_b3, tm=bd, alpha=alpha)
        xs = ffn(xs, mod_s, 2, 1, bd)

        rows = min(SPAN, s)
        st_p = [u_tail[:, SUBLANES - (CONV_WIDTH - 1):],
                p3[:, s - rows:, off_bk:off_bk + swa_kv].reshape(b, rows, n_kv, HEAD_DIM),
                p3[:, s - rows:, off_bv:off_bv + swa_kv].reshape(b, rows, n_kv, HEAD_DIM)]
        for gi, r in enumerate(DILATIONS):
            rows = min(SPAN * r, s)
            for off in (off_ck, off_cv):
                c0 = off + gi * dil_w
                st_p.append(p3[:, s - rows:, c0:c0 + dil_w].reshape(b, rows, n_dh, HEAD_DIM))
        for acc, v in zip(new_p, st_p):
            acc.append(v)

    P = [jnp.stack(a, 0) for a in new_p]
    S = [jnp.stack(conv_s, 0)]
    for ko, vo in dec_acc:
        S += [_keys_on_rows(ko), _keys_on_rows(vo)]
    S += [_keys_on_rows(c.reshape(depth, bd, n_dh, HEAD_DIM, -1)) for c in host_acc]
    out = [xp.reshape(b, s, d), xs.reshape(bd, 1, d)]
    for pa, sa in zip(P, S):
        out += [pa, sa]
    return tuple(out)
```

```python
import functools
from typing import NamedTuple

import jax
import jax.numpy as jnp
from jax import lax
from jax.experimental import pallas as pl
from jax.experimental.pallas import tpu as pltpu

HEAD_DIM = 64
SPAN = 128
CONV_WIDTH = 3
N_SUB = 3
DILATIONS = (1, 4, 16)
LN_EPS = 1e-5
SCALE = HEAD_DIM ** -0.5
SUBLANES = 8
LANES = 128
VMEM_LIMIT_BYTES = 58 * 1024 * 1024
PROJ_HOST_ELEMS = 2

F32 = jnp.float32
BF16 = jnp.bfloat16


def _params(*sem):
    return pltpu.CompilerParams(dimension_semantics=sem, vmem_limit_bytes=VMEM_LIMIT_BYTES)


def _largest_tile(n, cap):
    return max(t for t in range(128, cap + 1, 128) if n % t == 0)


def _dot(a, b):
    return jnp.dot(a, b, preferred_element_type=F32)


def _dot_nt(a, b):
    return lax.dot_general(a, b, (((1,), (1,)), ((), ())), preferred_element_type=F32)


def _layer_norm(z, g, b):
    mu = jnp.mean(z, axis=-1, keepdims=True)
    zc = z - mu
    var = jnp.mean(zc * zc, axis=-1, keepdims=True)
    return zc * lax.rsqrt(var + LN_EPS) * g + b


def _silu(x):
    return x * jax.nn.sigmoid(x)


def _mod_specs(mod, layer, cols, tiles_per_group, d):
    if mod.ndim == 4:
        return [pl.BlockSpec((None, None, 1, d), lambda i, *_, c=c: (layer, i // tiles_per_group, 0, c))
                for c in cols]
    return [pl.BlockSpec((None, mod.shape[1], d), lambda i, *_, c=c: (layer, 0, c)) for c in cols]


def _ln_spec(layer, sub, d):
    return pl.BlockSpec((None, 1, d), lambda *_: (layer * N_SUB + sub, 0, 0))


def _ada_kernel(c_ref, w_ref, b_ref, os_ref, op_ref, *, bd, b):
    a = _silu(c_ref[...]).astype(BF16)
    y = _dot(a, w_ref[...].astype(BF16)) + b_ref[...]
    os_ref[...] = y[0:bd]
    for i in range(b):
        op_ref[i] = y[bd + i:bd + i + 1]


def _ada(c_rows, bd, b, w_ada, b_ada, tn):
    depth, d, n = w_ada.shape
    r = c_rows.shape[0]
    return pl.pallas_call(
        functools.partial(_ada_kernel, bd=bd, b=b),
        grid=(depth, n // tn),
        in_specs=[pl.BlockSpec((r, d), lambda l, j: (0, 0)),
                  pl.BlockSpec((None, d, tn), lambda l, j: (l, 0, j)),
                  pl.BlockSpec((None, 1, tn), lambda l, j: (l, 0, j))],
        out_specs=[pl.BlockSpec((None, bd, tn), lambda l, j: (l, 0, j)),
                   pl.BlockSpec((None, b, 1, tn), lambda l, j: (l, 0, 0, j))],
        out_shape=[jax.ShapeDtypeStruct((depth, bd, n), F32), jax.ShapeDtypeStruct((depth, b, 1, n), F32)],
        compiler_params=_params("arbitrary", "arbitrary"),
        name="adaln",
    )(c_rows, w_ada, b_ada.reshape(depth, 1, n))


class _DecodeHost(NamedTuple):
    qt: jax.Array
    knt: jax.Array
    vnt: jax.Array
    k_view: jax.Array
    v_view: jax.Array
    k_acc: jax.Array | None
    v_acc: jax.Array | None
    layer: int
    first_elem: int
    n_elems: int
    dil: int


class _HostStatic(NamedTuple):
    parts: int
    first_elem: int
    n_elems: int
    dil: int
    aliased: bool


def _ffn_kernel(*refs, alpha, res_w, host):
    x_ref, sh_ref, sc_ref, gt_ref, wg_ref, wu_ref, wd_ref, lg_ref, lb_ref = refs[:9]
    pos = 9
    if host is not None:
        qt_ref, knt_ref, vnt_ref, k_ref, v_ref = refs[pos:pos + 5]
        pos += 7 if host.aliased else 5
    o_ref = refs[pos]
    pos += 1
    if host is not None:
        ko_ref, vo_ref, ot_ref, lt_ref = refs[pos:pos + 4]
        pos += 4
    h_ref = refs[pos]
    i = pl.program_id(0)
    f = pl.program_id(1)

    @pl.when(f == 0)
    def _():
        h_ref[...] = (x_ref[...] * (1.0 + sc_ref[...]) + sh_ref[...]).astype(BF16)
        o_ref[...] = jnp.zeros_like(o_ref)

    if host is not None:
        @pl.when((i == 0) & (f == 0))
        def _():
            ot_ref[...] = jnp.zeros_like(ot_ref)
            lt_ref[...] = jnp.zeros_like(lt_ref)

        unit = jnp.minimum(i * pl.num_programs(1) + f, host.n_elems * host.parts - 1)
        part = unit % host.parts
        ot, lt = _decode_unit(host.first_elem + unit // host.parts, qt_ref[part], knt_ref[part], vnt_ref[part], None,
                              k_ref, v_ref, ko_ref, vo_ref, ot_ref[part], lt_ref[part], group=1, dil=host.dil)
        ot_ref[part] = ot
        lt_ref[part] = lt

    h = h_ref[...]
    g = _dot(h, wg_ref[...])
    u = _dot(h, wu_ref[...])
    o_ref[...] += _dot((_silu(g) * u).astype(BF16), wd_ref[...])

    @pl.when(f == pl.num_programs(1) - 1)
    def _():
        z = alpha * x_ref[...] + res_w * gt_ref[...] * o_ref[...]
        o_ref[...] = _layer_norm(z, lg_ref[...], lb_ref[...])


def _ffn(x, mod, layer, sub, which, wg, wu, wd, ln_g, ln_b, *, tm, tf, alpha, res_w, host=None):
    m, d = x.shape
    f_dim = wg.shape[-1]
    groups = mod.shape[1] if mod.ndim == 4 else 1
    tpg = (m // groups) // tm
    nf = f_dim // tf
    in_specs = ([pl.BlockSpec((tm, d), lambda i, j: (i, 0))]
                + _mod_specs(mod, layer, (3 * sub, 3 * sub + 1, 3 * sub + 2), tpg, d)
                + [pl.BlockSpec((None, None, d, tf), lambda i, j: (layer, which, 0, j)),
                   pl.BlockSpec((None, None, d, tf), lambda i, j: (layer, which, 0, j)),
                   pl.BlockSpec((None, None, tf, d), lambda i, j: (layer, which, j, 0)),
                   _ln_spec(layer, sub, d), _ln_spec(layer, sub, d)])
    args = [x, mod, mod, mod, wg, wu, wd, ln_g, ln_b]
    out_specs = [pl.BlockSpec((tm, d), lambda i, j: (i, 0))]
    out_shape = [jax.ShapeDtypeStruct((m, d), F32)]
    aliases = {}
    static = None
    if host is not None:
        parts, rows, bd = host.qt.shape
        heads, _, w = host.k_view.shape[3:]
        static = _HostStatic(parts, host.first_elem, host.n_elems, host.dil, host.k_acc is not None)
        assert m // tm * nf >= host.n_elems * parts

        def unit_block(i, j):
            unit = jnp.minimum(i * nf + j, host.n_elems * parts - 1)
            return (host.layer, host.first_elem + unit // parts, unit % parts, 0, 0, 0)

        cols = pl.BlockSpec((parts, rows, bd), lambda i, j: (0, 0, 0))
        cache = pl.BlockSpec((None, None, None, heads, HEAD_DIM, w), unit_block)
        in_specs += [cols, cols, cols, cache, cache]
        args += [host.qt, host.knt, host.vnt, host.k_view, host.v_view]
        if host.k_acc is not None:
            aliases = {len(args): 1, len(args) + 1: 2}
            in_specs += [pl.BlockSpec(memory_space=pl.ANY), pl.BlockSpec(memory_space=pl.ANY)]
            args += [host.k_acc, host.v_acc]
        lse = pl.BlockSpec((parts, SUBLANES, bd), lambda i, j: (0, 0, 0))
        out_specs += [cache, cache, cols, lse]
        out_shape += [jax.ShapeDtypeStruct(host.k_view.shape, F32), jax.ShapeDtypeStruct(host.v_view.shape, F32),
                      jax.ShapeDtypeStruct((parts, rows, bd), F32), jax.ShapeDtypeStruct((parts, SUBLANES, bd), F32)]
    out = pl.pallas_call(
        functools.partial(_ffn_kernel, alpha=alpha, res_w=res_w, host=static),
        grid=(m // tm, nf),
        in_specs=in_specs,
        out_specs=out_specs,
        out_shape=out_shape,
        input_output_aliases=aliases,
        scratch_shapes=[pltpu.VMEM((tm, d), BF16)],
        compiler_params=_params("arbitrary", "arbitrary"),
        name="ffn" if host is None else "ffn_host",
    )(*args)
    return out[0] if host is None else out


class _GroupHost(NamedTuple):
    qt: jax.Array
    knt: jax.Array
    vnt: jax.Array
    sink: jax.Array | None
    k_cache: jax.Array
    v_cache: jax.Array
    k_acc: jax.Array | None
    v_acc: jax.Array | None
    dil: int


class _GroupStatic(NamedTuple):
    group: int
    dil: int
    has_sink: bool
    aliased: bool


def _proj_kernel(*refs, hosts, elems, n_units):
    x_ref, sh_ref, sc_ref, w_ref = refs[:4]
    pos = 4
    host_in = []
    for hs in hosts:
        n = 5 + hs.has_sink
        host_in.append(refs[pos:pos + n])
        pos += n + (2 if hs.aliased else 0)
    o_ref = refs[pos]
    pos += 1
    host_out = [refs[pos + 4 * g:pos + 4 * g + 4] for g in range(len(hosts))]
    h_ref = refs[pos + 4 * len(hosts)]
    i = pl.program_id(0)
    j = pl.program_id(1)

    @pl.when(j == 0)
    def _():
        h_ref[...] = (x_ref[...] * (1.0 + sc_ref[...]) + sh_ref[...]).astype(BF16)

    if hosts:
        @pl.when((i == 0) & (j == 0))
        def _():
            for _, _, ot_ref, lt_ref in host_out:
                ot_ref[...] = jnp.zeros_like(ot_ref)
                lt_ref[...] = jnp.zeros_like(lt_ref)

        unit = jnp.minimum(i * pl.num_programs(1) + j, n_units - 1)
        for hs, ins, (ko_ref, vo_ref, ot_ref, lt_ref) in zip(hosts, host_in, host_out):
            qt_ref, knt_ref, vnt_ref = ins[:3]
            sink = ins[3][...] if hs.has_sink else None
            k_ref, v_ref = ins[-2:]
            for e in range(elems):
                ot, lt = _decode_unit(unit * elems + e, qt_ref[...], knt_ref[...], vnt_ref[...], sink,
                                      k_ref.at[e], v_ref.at[e], ko_ref.at[e], vo_ref.at[e], ot_ref[...], lt_ref[...],
                                      group=hs.group, dil=hs.dil)
                ot_ref[...] = ot
                lt_ref[...] = lt

    o_ref[...] = _dot(h_ref[...], w_ref[...])


def _proj(x, mod, layer, w_in, *, tm, tn, hosts=(), elems=1):
    m, d = x.shape
    n = w_in.shape[-1]
    groups = mod.shape[1] if mod.ndim == 4 else 1
    tpg = (m // groups) // tm
    nj = n // tn
    in_specs = ([pl.BlockSpec((tm, d), lambda i, j: (i, 0))] + _mod_specs(mod, layer, (3, 4), tpg, d)
                + [pl.BlockSpec((None, d, tn), lambda i, j: (layer, 0, j))])
    args = [x, mod, mod, w_in]
    out_specs = [pl.BlockSpec((tm, tn), lambda i, j: (i, j))]
    out_shape = [jax.ShapeDtypeStruct((m, n), F32)]
    aliases = {}
    statics = []
    n_units = 0
    for host in hosts:
        depth, bd, n_kv, _, w = host.k_cache.shape
        n_q = host.qt.shape[0] // HEAD_DIM
        n_units = bd // elems
        assert m // tm * nj >= n_units
        statics.append(_GroupStatic(n_q // n_kv, host.dil, host.sink is not None, host.k_acc is not None))
        full = lambda shape: pl.BlockSpec(shape, lambda i, j: (0,) * len(shape))
        cache = pl.BlockSpec((None, elems, n_kv, HEAD_DIM, w),
                             lambda i, j: (layer, jnp.minimum(i * nj + j, n_units - 1), 0, 0, 0))
        in_specs += [full(host.qt.shape), full(host.knt.shape), full(host.vnt.shape)]
        args += [host.qt, host.knt, host.vnt]
        if host.sink is not None:
            in_specs.append(pl.BlockSpec((None, n_q, 1), lambda i, j: (layer, 0, 0)))
            args.append(host.sink)
        in_specs += [cache, cache]
        args += [host.k_cache, host.v_cache]
        if host.k_acc is not None:
            aliases[len(args)] = len(out_shape)
            aliases[len(args) + 1] = len(out_shape) + 1
            in_specs += [pl.BlockSpec(memory_space=pl.ANY), pl.BlockSpec(memory_space=pl.ANY)]
            args += [host.k_acc, host.v_acc]
        out_specs += [cache, cache, full(host.qt.shape), full((n_q, bd))]
        out_shape += [jax.ShapeDtypeStruct(host.k_cache.shape, F32), jax.ShapeDtypeStruct(host.v_cache.shape, F32),
                      jax.ShapeDtypeStruct(host.qt.shape, F32), jax.ShapeDtypeStruct((n_q, bd), F32)]
    out = pl.pallas_call(
        functools.partial(_proj_kernel, hosts=tuple(statics), elems=elems, n_units=n_units),
        grid=(m // tm, nj),
        in_specs=in_specs,
        out_specs=out_specs,
        out_shape=out_shape,
        input_output_aliases=aliases,
        scratch_shapes=[pltpu.VMEM((tm, d), BF16)],
        compiler_params=_params("arbitrary", "arbitrary"),
        name="proj_host" if hosts else "proj_in",
    )(*args)
    return out if hosts else out[0]


def _band_kernel(*refs, n_heads, group, dil, has_sink, has_lse):
    q_ref, kp_ref, kc_ref, vp_ref, vc_ref = refs[:5]
    pos = 5
    sink_ref = None
    if has_sink:
        sink_ref = refs[pos]
        pos += 1
    o_ref = refs[pos]
    lse_ref = refs[pos + 1] if has_lse else None
    pos += 2 if has_lse else 1

    first_chunk = pl.program_id(1) == 0
    qi = lax.broadcasted_iota(jnp.int32, (SPAN, 2 * SPAN), 0)
    ki = lax.broadcasted_iota(jnp.int32, (SPAN, 2 * SPAN), 1)
    dist = qi + SPAN - ki
    valid = (dist >= 0) & (dist <= SPAN) & ((ki >= SPAN) | jnp.logical_not(first_chunk))

    if dil > 1:
        q_t, kp_t, kc_t, vp_t, vc_t, o_t, lse_t = refs[pos:pos + 7]
        for src, dst in ((q_ref, q_t), (kp_ref, kp_t), (kc_ref, kc_t), (vp_ref, vp_t), (vc_ref, vc_t)):
            for c in range(dst.shape[0]):
                dst[c] = src[:, c * LANES:(c + 1) * LANES]

    def read(ref, tiled, rho):
        if dil == 1:
            return ref[...]
        return jnp.concatenate([tiled[c, pl.ds(rho, SPAN, stride=dil), :] for c in range(tiled.shape[0])], axis=1)

    def write(ref, tiled, rho, val):
        if dil == 1:
            ref[...] = val
        else:
            for c in range(tiled.shape[0]):
                tiled[c, pl.ds(rho, SPAN, stride=dil), :] = val[:, c * LANES:(c + 1) * LANES]

    lane = lax.broadcasted_iota(jnp.int32, (1, LANES), 1)
    half_mask = (lane < HEAD_DIM, lane >= HEAD_DIM)
    ones = jnp.ones((2 * SPAN, LANES), BF16)

    def residue(rho, carry):
        q = read(q_ref, q_t if dil > 1 else None, rho)
        k = jnp.concatenate([read(kp_ref, kp_t if dil > 1 else None, rho),
                             read(kc_ref, kc_t if dil > 1 else None, rho)], axis=0)
        v = jnp.concatenate([read(vp_ref, vp_t if dil > 1 else None, rho),
                             read(vc_ref, vc_t if dil > 1 else None, rho)], axis=0)
        n_qt = n_heads // 2
        o_half = [[None, None] for _ in range(n_qt)]
        l_half = [[None, None] for _ in range(n_qt)]
        for h in range(n_heads // group):
            k_tile = k[:, (h // 2) * LANES:(h // 2 + 1) * LANES]
            v_tile = v[:, (h // 2) * LANES:(h // 2 + 1) * LANES]
            for half in (0, 1):
                hqs = [hq for hq in range(h * group, (h + 1) * group) if hq % 2 == half]
                if not hqs:
                    continue
                aligned = (h % 2) == half
                kh = (k_tile if aligned else pltpu.roll(k_tile, HEAD_DIM, 1)).astype(BF16)
                vh = (v_tile if aligned else pltpu.roll(v_tile, HEAD_DIM, 1)).astype(BF16)
                qg = jnp.concatenate(
                    [jnp.where(half_mask[half], q[:, (hq // 2) * LANES:(hq // 2 + 1) * LANES], 0.0) for hq in hqs],
                    axis=0).astype(BF16)
                s = _dot_nt(qg, kh) * SCALE
                s = jnp.where(jnp.concatenate([valid] * len(hqs), axis=0) if len(hqs) > 1 else valid, s, -jnp.inf)
                m = jnp.max(s, axis=-1, keepdims=True)
                if has_sink:
                    sk = jnp.concatenate([jnp.broadcast_to(sink_ref[:, hq:hq + 1], (SPAN, 1)) for hq in hqs], axis=0)
                    m = jnp.maximum(m, sk)
                e = jnp.exp(s - m).astype(BF16)
                ov = _dot(e, jnp.concatenate([vh, ones], axis=1))
                den = ov[:, LANES:]
                if has_sink:
                    den = den + jnp.exp(sk - m)
                og = ov[:, :LANES] / den
                lg = m + jnp.log(den)
                for i, hq in enumerate(hqs):
                    o_half[hq // 2][half] = og[i * SPAN:(i + 1) * SPAN]
                    l_half[hq // 2][half] = lg[i * SPAN:(i + 1) * SPAN]
        write(o_ref, o_t if dil > 1 else None, rho,
              jnp.concatenate([jnp.where(half_mask[0], a, b) for a, b in o_half], axis=1))
        if has_lse:
            write(lse_ref, lse_t if dil > 1 else None, rho,
                  jnp.concatenate([jnp.where(half_mask[0], a, b) for a, b in l_half], axis=1))
        return carry

    if dil > 1:
        lax.fori_loop(0, dil, residue, 0)
        for c in range(o_t.shape[0]):
            o_ref[:, c * LANES:(c + 1) * LANES] = o_t[c]
            if has_lse:
                lse_ref[:, c * LANES:(c + 1) * LANES] = lse_t[c]
    else:
        residue(0, 0)


def _band_attention(p3, dil, q_off, k_off, v_off, n_kv, group, sink, heads_per_step):
    b, s, n = p3.shape
    chunk = SPAN * dil
    nsteps = n_kv // heads_per_step
    qw, kw = heads_per_step * group * HEAD_DIM, heads_per_step * HEAD_DIM
    qspec = pl.BlockSpec((None, chunk, qw), lambda bi, c, hb: (bi, c, q_off // qw + hb))

    def kvspec(off, prev):
        if prev:
            return pl.BlockSpec((None, chunk, kw), lambda bi, c, hb: (bi, jnp.maximum(c - 1, 0), off // kw + hb))
        return pl.BlockSpec((None, chunk, kw), lambda bi, c, hb: (bi, c, off // kw + hb))

    in_specs = [qspec, kvspec(k_off, True), kvspec(k_off, False), kvspec(v_off, True), kvspec(v_off, False)]
    args = [p3] * 5
    has_sink = sink is not None
    if has_sink:
        in_specs.append(pl.BlockSpec((None, 1, qw // HEAD_DIM), lambda bi, c, hb: (hb, 0, 0)))
        args.append(sink.reshape(nsteps, 1, qw // HEAD_DIM))
    ospec = pl.BlockSpec((None, chunk, qw), lambda bi, c, hb: (bi, c, hb))
    oshape = jax.ShapeDtypeStruct((b, s, n_kv * group * HEAD_DIM), F32)
    has_lse = not has_sink
    scratch = []
    if dil > 1:
        tiled = lambda width: pltpu.VMEM((width // LANES, chunk, LANES), F32)
        scratch = [tiled(qw)] + [tiled(kw)] * 4 + [tiled(qw)] * 2
    return pl.pallas_call(
        functools.partial(_band_kernel, n_heads=heads_per_step * group, group=group, dil=dil,
                          has_sink=has_sink, has_lse=has_lse),
        grid=(b, s // chunk, nsteps),
        in_specs=in_specs,
        out_specs=[ospec, ospec] if has_lse else ospec,
        out_shape=[oshape, oshape] if has_lse else oshape,
        scratch_shapes=scratch,
        compiler_params=_params("arbitrary", "arbitrary", "arbitrary"),
        name=f"band_attn_r{dil}",
    )(*args)


def _conv_kernel(ax_ref, ab_ref, ac_ref, axp_ref, acp_ref, w_ref, y_ref, tail_ref, ue_ref, *, tm):
    i = pl.program_id(1)
    u = ac_ref[...] * ax_ref[...]
    halo = acp_ref[...] * axp_ref[...]
    ue_ref[0:SUBLANES, :] = jnp.where(i > 0, halo, 0.0)
    ue_ref[SUBLANES:, :] = u
    u1 = ue_ref[pl.ds(SUBLANES - 1, tm), :]
    u2 = ue_ref[pl.ds(SUBLANES - 2, tm), :]
    y_ref[...] = ab_ref[...] * (w_ref[0:1, :] * u2 + w_ref[1:2, :] * u1 + w_ref[2:3, :] * u)
    tail_ref[...] = u[tm - SUBLANES:, :]


def _conv_prompt(p3, layer, conv_w, a_w, tm):
    b, s, n = p3.shape
    cur = lambda c: pl.BlockSpec((None, tm, a_w), lambda bi, i, c=c: (bi, i, c))
    prev = lambda c: pl.BlockSpec((None, SUBLANES, a_w),
                                  lambda bi, i, c=c: (bi, jnp.maximum(i * (tm // SUBLANES) - 1, 0), c))
    return pl.pallas_call(
        functools.partial(_conv_kernel, tm=tm),
        grid=(b, s // tm),
        in_specs=[cur(0), cur(1), cur(2), prev(0), prev(2),
                  pl.BlockSpec((None, CONV_WIDTH, a_w), lambda bi, i: (layer, 0, 0))],
        out_specs=[pl.BlockSpec((None, tm, a_w), lambda bi, i: (bi, i, 0)),
                   pl.BlockSpec((None, SUBLANES, a_w), lambda bi, i: (bi, 0, 0))],
        out_shape=[jax.ShapeDtypeStruct((b, s, a_w), F32), jax.ShapeDtypeStruct((b, SUBLANES, a_w), F32)],
        scratch_shapes=[pltpu.VMEM((tm + SUBLANES, a_w), F32)],
        compiler_params=_params("arbitrary", "arbitrary"),
        name="short_conv",
    )(p3, p3, p3, p3, p3, conv_w)


def _conv_step_kernel(ax_ref, ab_ref, ac_ref, st_ref, cw_ref, ya_ref, st_out_ref):
    a_w = ax_ref.shape[1]
    u = ac_ref[...] * ax_ref[...]
    s0 = st_ref[:, 0:a_w]
    s1 = st_ref[:, a_w:2 * a_w]
    ya_ref[...] = ab_ref[...] * (cw_ref[0:1, :] * s0 + cw_ref[1:2, :] * s1 + cw_ref[2:3, :] * u)
    st_out_ref[:, 0:a_w] = s1
    st_out_ref[:, a_w:2 * a_w] = u


def _conv_step(ps, layer, conv_state2, conv_w, a_w):
    bd = ps.shape[0]
    col = lambda c: pl.BlockSpec((bd, a_w), lambda i, c=c: (0, c))
    return pl.pallas_call(
        _conv_step_kernel,
        grid=(1,),
        in_specs=[col(0), col(1), col(2),
                  pl.BlockSpec((None, bd, 2 * a_w), lambda i: (layer, 0, 0)),
                  pl.BlockSpec((None, CONV_WIDTH, a_w), lambda i: (layer, 0, 0))],
        out_specs=[pl.BlockSpec((bd, a_w), lambda i: (0, 0)), pl.BlockSpec((bd, 2 * a_w), lambda i: (0, 0))],
        out_shape=[jax.ShapeDtypeStruct((bd, a_w), F32), jax.ShapeDtypeStruct((bd, 2 * a_w), F32)],
        compiler_params=_params("arbitrary"),
        name="conv_step",
    )(ps, ps, ps, conv_state2, conv_w)


def _pick_column(x, onehot):
    return jnp.sum(jnp.where(onehot, x, 0.0), axis=1, keepdims=True)


def _decode_unit(elem, qt, knt, vnt, sink, k_ref, v_ref, ko_ref, vo_ref, ot_old, lt_old, *, group, dil):
    n_kv, _, w = k_ref.shape
    bd = qt.shape[1]
    n_q = n_kv * group
    lane_b = lax.broadcasted_iota(jnp.int32, (1, bd), 1)
    row_b = lax.broadcasted_iota(jnp.int32, (bd, LANES), 0)
    lane_w = lax.broadcasted_iota(jnp.int32, (1, w), 1)
    key_ok = (lane_w % dil) == 0
    last = lane_w == w - 1
    head = lambda x, i: x[i * HEAD_DIM:(i + 1) * HEAD_DIM]

    onehot = lane_b == elem
    q_all = _dot(qt.astype(BF16), (row_b == elem).astype(BF16))
    kn_all = _pick_column(knt, onehot)
    vn_all = _pick_column(vnt, onehot)
    s_rows, sn_rows = [], []
    for h in range(n_kv):
        k = k_ref[h]
        kn = head(kn_all, h)
        ko_ref[h] = jnp.where(last, kn, pltpu.roll(k, w - 1, 1))
        for g in range(group):
            q = head(q_all, h * group + g)
            q_wide = jnp.tile(q, (1, w // LANES)) if w > LANES else q
            s_rows.append(jnp.sum(k * q_wide, axis=0, keepdims=True))
            sn_rows.append(jnp.sum(kn * q[:, 0:1], axis=0, keepdims=True))
    s = jnp.where(key_ok, jnp.concatenate(s_rows, axis=0) * SCALE, -jnp.inf)
    s_new = jnp.concatenate(sn_rows, axis=0) * SCALE
    m = jnp.maximum(jnp.max(s, axis=1, keepdims=True), s_new)
    if sink is not None:
        m = jnp.maximum(m, sink)
    e = jnp.exp(s - m)
    e_new = jnp.exp(s_new - m)
    den = jnp.sum(e, axis=1, keepdims=True) + e_new
    if sink is not None:
        den = den + jnp.exp(sink - m)
    p = e / den
    p_new = e_new / den
    o_cols = []
    for h in range(n_kv):
        v = v_ref[h]
        vn = head(vn_all, h)
        vo_ref[h] = jnp.where(last, vn, pltpu.roll(v, w - 1, 1))
        for g in range(group):
            hq = h * group + g
            o_cols.append(jnp.sum(v * p[hq:hq + 1, :], axis=1, keepdims=True) + p_new[hq:hq + 1, :] * vn)
    lse = m + jnp.log(den)
    if lt_old.shape[0] > n_q:
        lse = jnp.concatenate([lse, jnp.zeros((lt_old.shape[0] - n_q, 1), F32)], axis=0)
    return (jnp.where(onehot, jnp.concatenate(o_cols, axis=0), ot_old), jnp.where(onehot, lse, lt_old))


def _dec_kernel(*refs, bb, group, dil, has_sink, aliased):
    qt_ref, knt_ref, vnt_ref = refs[:3]
    pos = 3
    sink_ref = None
    if has_sink:
        sink_ref = refs[pos]
        pos += 1
    k_ref, v_ref = refs[pos:pos + 2]
    pos += 4 if aliased else 2
    ko_ref, vo_ref, ot_ref, lt_ref = refs[pos:pos + 4]
    step = pl.program_id(0)

    @pl.when(step == 0)
    def _():
        ot_ref[...] = jnp.zeros_like(ot_ref)
        lt_ref[...] = jnp.zeros_like(lt_ref)

    def body(i, carry):
        ot, lt = _decode_unit(step * bb + i, qt_ref[...], knt_ref[...], vnt_ref[...],
                              sink_ref[...] if has_sink else None,
                              k_ref.at[i], v_ref.at[i], ko_ref.at[i], vo_ref.at[i], ot_ref[...], lt_ref[...],
                              group=group, dil=dil)
        ot_ref[...] = ot
        lt_ref[...] = lt
        return carry

    lax.fori_loop(0, bb, body, 0)


def _decode_group(qt, knt, vnt, sink, k_cache, v_cache, k_acc, v_acc, layer, dil, bb):
    depth, bd, n_kv, _, w = k_cache.shape
    hq = qt.shape[0] // HEAD_DIM
    full = lambda shape: pl.BlockSpec(shape, lambda i: (0, 0))
    cache = pl.BlockSpec((None, bb, n_kv, HEAD_DIM, w), lambda i: (layer, i, 0, 0, 0))
    in_specs = [full(qt.shape), full(knt.shape), full(vnt.shape)]
    args = [qt, knt, vnt]
    if sink is not None:
        in_specs.append(pl.BlockSpec((None, hq, 1), lambda i: (layer, 0, 0)))
        args.append(sink)
    in_specs += [cache, cache]
    args += [k_cache, v_cache]
    aliases = {}
    if k_acc is not None:
        aliases = {len(args): 0, len(args) + 1: 1}
        in_specs += [pl.BlockSpec(memory_space=pl.ANY), pl.BlockSpec(memory_space=pl.ANY)]
        args += [k_acc, v_acc]
    return pl.pallas_call(
        functools.partial(_dec_kernel, bb=bb, group=hq // n_kv, dil=dil,
                          has_sink=sink is not None, aliased=k_acc is not None),
        grid=(bd // bb,),
        in_specs=in_specs,
        out_specs=[cache, cache, full(qt.shape), full((hq, bd))],
        out_shape=[jax.ShapeDtypeStruct(k_cache.shape, F32), jax.ShapeDtypeStruct(v_cache.shape, F32),
                   jax.ShapeDtypeStruct(qt.shape, F32), jax.ShapeDtypeStruct((hq, bd), F32)],
        input_output_aliases=aliases,
        compiler_params=_params("arbitrary"),
        name=f"decode_w{w}",
    )(*args)


def _merge_kernel(*refs, alpha, n_lse, d):
    x_ref, gt_ref, ya_ref, yb_ref = refs[:4]
    pos = 4
    o_refs = refs[pos:pos + n_lse]
    l_refs = refs[pos + n_lse:pos + 2 * n_lse]
    pos += 2 * n_lse
    ls = [r[...] for r in l_refs]
    mx = functools.reduce(jnp.maximum, ls)
    ws = [jnp.exp(l - mx) for l in ls]
    tot = functools.reduce(lambda a, b: a + b, ws)
    yc = functools.reduce(lambda a, b: a + b, [(w / tot) * o[...] for w, o in zip(ws, o_refs)])
    g1_ref, g2_ref, wa_ref, wb_ref, wc_ref, wo_ref, lg_ref, lb_ref, out_ref = refs[pos:]
    hw = d // 2
    ga = g1_ref[:, 0:d]
    gb = jnp.concatenate([g1_ref[:, d:d + hw], g2_ref[:, 0:hw]], axis=1)
    gc = g2_ref[:, hw:hw + d]
    m = (jax.nn.sigmoid(ga) * _dot(ya_ref[...].astype(BF16), wa_ref[...])
         + jax.nn.sigmoid(gb) * _dot(yb_ref[...].astype(BF16), wb_ref[...])
         + jax.nn.sigmoid(gc) * _dot(yc.astype(BF16), wc_ref[...]))
    y = _dot(m.astype(BF16), wo_ref[...])
    z = alpha * x_ref[...] + gt_ref[...] * y
    out_ref[...] = _layer_norm(z, lg_ref[...], lb_ref[...])


def _merge(x, mod, layer, ya, yb, outs, lses, proj, g_off, wa, wb, wc, wo, ln_g, ln_b, *, tm, alpha):
    m, d = x.shape
    groups = mod.shape[1] if mod.ndim == 4 else 1
    tpg = (m // groups) // tm
    row = lambda w, c=0: pl.BlockSpec((tm, w), lambda i, c=c: (i, c))
    stacked = lambda a: pl.BlockSpec((None,) + a.shape[1:], lambda i: (layer, 0, 0))
    gw = 3 * d // 2
    in_specs = ([row(d)] + _mod_specs(mod, layer, (5,), tpg, d) + [row(ya.shape[1]), row(yb.shape[1])]
                + [row(o.shape[1]) for o in outs] + [row(l.shape[1]) for l in lses]
                + [row(gw, g_off // gw), row(gw, g_off // gw + 1), stacked(wa), stacked(wb), stacked(wc), stacked(wo),
                   _ln_spec(layer, 1, d), _ln_spec(layer, 1, d)])
    args = [x, mod, ya, yb] + list(outs) + list(lses) + [proj, proj, wa, wb, wc, wo, ln_g, ln_b]
    return pl.pallas_call(
        functools.partial(_merge_kernel, alpha=alpha, n_lse=len(outs), d=d),
        grid=(m // tm,),
        in_specs=in_specs,
        out_specs=row(d),
        out_shape=jax.ShapeDtypeStruct((m, d), F32),
        compiler_params=_params("arbitrary"),
        name="merge",
    )(*args)


def _keys_on_lanes(cache):
    return jnp.transpose(cache, (0, 1, 3, 4, 2))


def _keys_on_rows(cache_t):
    return jnp.transpose(cache_t, (0, 1, 4, 2, 3))


def kernel(x_prompt, x_sample, state_conv, cache_swa_k, cache_swa_v, cache_dil0_k, cache_dil0_v,
           cache_dil1_k, cache_dil1_v, cache_dil2_k, cache_dil2_v, c_prompt, c_sample, w_ada, b_ada,
           ln_g, ln_b, ffn_w_gate, ffn_w_up, ffn_w_down, w_in, conv_w, attn_sink,
           w_br_a, w_br_b, w_br_c, w_out):
    b, s, d = x_prompt.shape
    bd = x_sample.shape[0]
    depth = w_in.shape[0]
    a_w = w_br_a.shape[1]
    swa_q = w_br_b.shape[1]
    dil_w = w_br_c.shape[1]
    swa_kv = cache_swa_k.shape[3] * HEAD_DIM
    n_kv = swa_kv // HEAD_DIM
    swa_group = swa_q // swa_kv
    n_dh = dil_w // HEAD_DIM
    dq = len(DILATIONS) * dil_w
    alpha = (2.0 * depth) ** 0.25
    off_bq = 3 * a_w
    off_bk = off_bq + swa_q
    off_bv = off_bk + swa_kv
    off_cq = off_bv + swa_kv
    off_ck = off_cq + dq
    off_cv = off_ck + dq
    off_g = off_cv + dq
    f_dim = ffn_w_gate.shape[-1]

    tm_ffn = min(512, s)
    tm_proj = min(512, s)
    tf = _largest_tile(f_dim, 512)
    tn_in = _largest_tile(w_in.shape[-1], 1024)
    tn_in_prompt = _largest_tile(w_in.shape[-1], 3072)
    tn_ada = _largest_tile(w_ada.shape[-1], 1024)
    tm_merge = min(256, s)
    tm_conv = min(512, s)
    dil_heads_per_step = tuple(max(LANES // HEAD_DIM, n_dh // max(1, r // 8)) for r in DILATIONS)

    pad = (-(b + bd)) % SUBLANES
    c_rows = jnp.concatenate([c_sample, c_prompt, jnp.zeros((pad, d), F32)], axis=0)
    mod_s, mod_p = _ada(c_rows, bd, b, w_ada, b_ada, tn_ada)

    wg_b, wu_b, wd_b = ffn_w_gate.astype(BF16), ffn_w_up.astype(BF16), ffn_w_down.astype(BF16)
    w_in_b = w_in.astype(BF16)
    wa_b, wb_b, wc_b, wo_b = (w.astype(BF16) for w in (w_br_a, w_br_b, w_br_c, w_out))
    ln_g3 = ln_g.reshape(depth * N_SUB, 1, d)
    ln_b3 = ln_b.reshape(depth * N_SUB, 1, d)
    sink3 = attn_sink.reshape(depth, -1, 1)

    dec_groups = [(_keys_on_lanes(cache_swa_k), _keys_on_lanes(cache_swa_v), 1, None),
                  (_keys_on_lanes(cache_dil0_k), _keys_on_lanes(cache_dil0_v), DILATIONS[0], None),
                  (_keys_on_lanes(cache_dil1_k), _keys_on_lanes(cache_dil1_v), DILATIONS[1], 4)]
    dec_acc = [(None, None)] * len(dec_groups)
    host_parts = 2
    host_heads = n_dh // host_parts
    host_shape = (depth, bd, host_parts, host_heads, HEAD_DIM, cache_dil2_k.shape[2])
    host_k = _keys_on_lanes(cache_dil2_k).reshape(host_shape)
    host_v = _keys_on_lanes(cache_dil2_v).reshape(host_shape)
    host_acc = (None, None)
    host_gi = len(DILATIONS) - 1
    conv2 = state_conv.reshape(depth, bd, 2 * a_w)

    xp = x_prompt.reshape(b * s, d)
    xs = x_sample.reshape(bd, d)
    new_p = [[] for _ in range(9)]
    conv_s = []
    for l in range(depth):
        ffn = lambda x, mod, sub, which, tm, host=None: _ffn(x, mod, l, sub, which, wg_b, wu_b, wd_b, ln_g3, ln_b3,
                                                             tm=tm, tf=tf, alpha=alpha, res_w=0.5, host=host)
        xs = ffn(xs, mod_s, 0, 0, bd)
        ps = _proj(xs, mod_s, l, w_in_b, tm=bd, tn=tn_in)
        ya_s, st_s = _conv_step(ps, l, conv2, conv_w, a_w)
        conv_s.append(st_s.reshape(bd, CONV_WIDTH - 1, a_w))
        pst = ps[:, off_bq:off_g].T
        col = lambda off, width: pst[off - off_bq:off - off_bq + width]
        outs_s, lses_s = [None] * len(dec_groups), [None] * len(dec_groups)
        proj_hosts = []
        for gidx, (kc, vc, dil, bb) in enumerate(dec_groups):
            if gidx == 0:
                qt, knt, vnt, sink = col(off_bq, swa_q), col(off_bk, swa_kv), col(off_bv, swa_kv), sink3
            else:
                gi = gidx - 1
                qt, knt, vnt, sink = (col(off_cq + gi * dil_w, dil_w), col(off_ck + gi * dil_w, dil_w),
                                      col(off_cv + gi * dil_w, dil_w), None)
            if bb is None:
                proj_hosts.append((gidx, _GroupHost(qt, knt, vnt, sink, kc, vc, *dec_acc[gidx], dil)))
                continue
            ko, vo, ot, lt = _decode_group(qt, knt, vnt, sink, kc, vc, *dec_acc[gidx], l, dil, bb)
            dec_acc[gidx] = (ko, vo)
            outs_s[gidx] = ot.T
            lses_s[gidx] = jnp.repeat(lt.T, HEAD_DIM, axis=1)
        host_cols = [col(off + host_gi * dil_w, dil_w).reshape(host_parts, host_heads * HEAD_DIM, bd)
                     for off in (off_cq, off_ck, off_cv)]
        make_host = lambda first: _DecodeHost(*host_cols, host_k, host_v, *host_acc, l, first, bd // 2,
                                              DILATIONS[host_gi])
        xp, hk, hv, ot_a, lt_a = ffn(xp, mod_p, 0, 0, tm_ffn, make_host(0))
        host_acc = (hk, hv)
        pp, *hosted = _proj(xp, mod_p, l, w_in_b, tm=tm_proj, tn=tn_in_prompt,
                            hosts=tuple(h for _, h in proj_hosts), elems=PROJ_HOST_ELEMS)
        for n, (gidx, _) in enumerate(proj_hosts):
            ko, vo, ot, lt = hosted[4 * n:4 * n + 4]
            dec_acc[gidx] = (ko, vo)
            outs_s[gidx] = ot.T
            lses_s[gidx] = jnp.repeat(lt.T, HEAD_DIM, axis=1)
        p3 = pp.reshape(b, s, -1)
        ya_p, u_tail = _conv_prompt(p3, l, conv_w, a_w, tm_conv)
        yb_p = _band_attention(p3, 1, off_bq, off_bk, off_bv, n_kv, swa_group, attn_sink[l], n_kv)
        outs, lses = [], []
        for gi, r in enumerate(DILATIONS):
            og, lg = _band_attention(p3, r, off_cq + gi * dil_w, off_ck + gi * dil_w, off_cv + gi * dil_w,
                                     n_dh, 1, None, dil_heads_per_step[gi])
            outs.append(og.reshape(b * s, dil_w))
            lses.append(lg.reshape(b * s, dil_w))
        xp = _merge(xp, mod_p, l, ya_p.reshape(b * s, a_w), yb_p.reshape(b * s, swa_q), outs, lses, pp, off_g,
                    wa_b, wb_b, wc_b, wo_b, ln_g3, ln_b3, tm=tm_merge, alpha=alpha)
        xp, hk, hv, ot_b, lt_b = ffn(xp, mod_p, 2, 1, tm_ffn, make_host(bd // 2))
        host_acc = (hk, hv)
        outs_s.append((ot_a + ot_b).reshape(dil_w, bd).T)
        lt_h = (lt_a + lt_b)[:, :host_heads].reshape(n_dh, bd)
        lses_s.append(jnp.repeat(lt_h.T, HEAD_DIM, axis=1))
        xs = _merge(xs, mod_s, l, ya_s, outs_s[0], outs_s[1:], lses_s[1:], ps, off_g,
                    wa_b, wb_b, wc_b, wo_b, ln_g3, ln_b3, tm=bd, alpha=alpha)
        xs = ffn(xs, mod_s, 2, 1, bd)

        rows = min(SPAN, s)
        st_p = [u_tail[:, SUBLANES - (CONV_WIDTH - 1):],
                p3[:, s - rows:, off_bk:off_bk + swa_kv].reshape(b, rows, n_kv, HEAD_DIM),
                p3[:, s - rows:, off_bv:off_bv + swa_kv].reshape(b, rows, n_kv, HEAD_DIM)]
        for gi, r in enumerate(DILATIONS):
            rows = min(SPAN * r, s)
            for off in (off_ck, off_cv):
                c0 = off + gi * dil_w
                st_p.append(p3[:, s - rows:, c0:c0 + dil_w].reshape(b, rows, n_dh, HEAD_DIM))
        for acc, v in zip(new_p, st_p):
            acc.append(v)

    P = [jnp.stack(a, 0) for a in new_p]
    S = [jnp.stack(conv_s, 0)]
    for ko, vo in dec_acc:
        S += [_keys_on_rows(ko), _keys_on_rows(vo)]
    S += [_keys_on_rows(c.reshape(depth, bd, n_dh, HEAD_DIM, -1)) for c in host_acc]
    out = [xp.reshape(b, s, d), xs.reshape(bd, 1, d)]
    for pa, sa in zip(P, S):
        out += [pa, sa]
    return tuple(out)
```

```python
import functools
from typing import NamedTuple

import jax
import jax.numpy as jnp
from jax import lax
from jax.experimental import pallas as pl
from jax.experimental.pallas import tpu as pltpu

HEAD_DIM = 64
SPAN = 128
CONV_WIDTH = 3
N_SUB = 3
DILATIONS = (1, 4, 16)
LN_EPS = 1e-5
SCALE = HEAD_DIM ** -0.5
SUBLANES = 8
LANES = 128
VMEM_LIMIT_BYTES = 58 * 1024 * 1024

F32 = jnp.float32
BF16 = jnp.bfloat16


def _params(*sem):
    return pltpu.CompilerParams(dimension_semantics=sem, vmem_limit_bytes=VMEM_LIMIT_BYTES)


def _largest_tile(n, cap):
    return max(t for t in range(128, cap + 1, 128) if n % t == 0)


def _dot(a, b):
    return jnp.dot(a, b, preferred_element_type=F32)


def _dot_nt(a, b):
    return lax.dot_general(a, b, (((1,), (1,)), ((), ())), preferred_element_type=F32)


def _layer_norm(z, g, b):
    mu = jnp.mean(z, axis=-1, keepdims=True)
    zc = z - mu
    var = jnp.mean(zc * zc, axis=-1, keepdims=True)
    return zc * lax.rsqrt(var + LN_EPS) * g + b


def _silu(x):
    return x * jax.nn.sigmoid(x)


def _mod_specs(mod, layer, cols, tiles_per_group, d):
    if mod.ndim == 4:
        return [pl.BlockSpec((None, None, 1, d), lambda i, *_, c=c: (layer, i // tiles_per_group, 0, c))
                for c in cols]
    return [pl.BlockSpec((None, mod.shape[1], d), lambda i, *_, c=c: (layer, 0, c)) for c in cols]


def _ln_spec(layer, sub, d):
    return pl.BlockSpec((None, 1, d), lambda *_: (layer * N_SUB + sub, 0, 0))


CAST_BLOCK_BYTES = 8 * 1024 * 1024


def _cast_kernel(x_ref, o_ref):
    o_ref[...] = x_ref[...].astype(BF16)


def _to_bf16(w):
    r, c = w.shape[-2:]
    w3 = w.reshape((-1, r, c))
    tr = max(t for t in range(16, r + 1, 16) if r % t == 0 and t * c * 4 <= CAST_BLOCK_BYTES)
    out = pl.pallas_call(
        _cast_kernel,
        grid=(w3.shape[0], r // tr),
        in_specs=[pl.BlockSpec((None, tr, c), lambda a, i: (a, i, 0))],
        out_specs=pl.BlockSpec((None, tr, c), lambda a, i: (a, i, 0)),
        out_shape=jax.ShapeDtypeStruct(w3.shape, BF16),
        compiler_params=_params("arbitrary", "arbitrary"),
        name="to_bf16",
    )(w3)
    return out.reshape(w.shape)


def _ada_kernel(c_ref, w_ref, b_ref, os_ref, op_ref, *, bd, b):
    a = _silu(c_ref[...]).astype(BF16)
    y = _dot(a, w_ref[...].astype(BF16)) + b_ref[...]
    os_ref[...] = y[0:bd]
    for i in range(b):
        op_ref[i] = y[bd + i:bd + i + 1]


def _ada(c_rows, bd, b, w_ada, b_ada, tn):
    depth, d, n = w_ada.shape
    r = c_rows.shape[0]
    return pl.pallas_call(
        functools.partial(_ada_kernel, bd=bd, b=b),
        grid=(depth, n // tn),
        in_specs=[pl.BlockSpec((r, d), lambda l, j: (0, 0)),
                  pl.BlockSpec((None, d, tn), lambda l, j: (l, 0, j)),
                  pl.BlockSpec((None, 1, tn), lambda l, j: (l, 0, j))],
        out_specs=[pl.BlockSpec((None, bd, tn), lambda l, j: (l, 0, j)),
                   pl.BlockSpec((None, b, 1, tn), lambda l, j: (l, 0, 0, j))],
        out_shape=[jax.ShapeDtypeStruct((depth, bd, n), F32), jax.ShapeDtypeStruct((depth, b, 1, n), F32)],
        compiler_params=_params("arbitrary", "arbitrary"),
        name="adaln",
    )(c_rows, w_ada, b_ada.reshape(depth, 1, n))


class _DecodeHost(NamedTuple):
    qt: jax.Array
    knt: jax.Array
    vnt: jax.Array
    k_view: jax.Array
    v_view: jax.Array
    k_acc: jax.Array | None
    v_acc: jax.Array | None
    layer: int
    first_elem: int
    n_elems: int
    dil: int


class _HostStatic(NamedTuple):
    parts: int
    first_elem: int
    n_elems: int
    dil: int
    aliased: bool


def _ffn_kernel(*refs, alpha, res_w, host):
    x_ref, sh_ref, sc_ref, gt_ref, wg_ref, wu_ref, wd_ref, lg_ref, lb_ref = refs[:9]
    pos = 9
    if host is not None:
        qt_ref, knt_ref, vnt_ref, k_ref, v_ref = refs[pos:pos + 5]
        pos += 7 if host.aliased else 5
    o_ref = refs[pos]
    pos += 1
    if host is not None:
        ko_ref, vo_ref, ot_ref, lt_ref = refs[pos:pos + 4]
        pos += 4
    h_ref = refs[pos]
    i = pl.program_id(0)
    f = pl.program_id(1)

    @pl.when(f == 0)
    def _():
        h_ref[...] = (x_ref[...] * (1.0 + sc_ref[...]) + sh_ref[...]).astype(BF16)
        o_ref[...] = jnp.zeros_like(o_ref)

    if host is not None:
        @pl.when((i == 0) & (f == 0))
        def _():
            ot_ref[...] = jnp.zeros_like(ot_ref)
            lt_ref[...] = jnp.zeros_like(lt_ref)

        unit = jnp.minimum(i * pl.num_programs(1) + f, host.n_elems * host.parts - 1)
        part = unit % host.parts
        ot, lt = _decode_unit(host.first_elem + unit // host.parts, qt_ref[part], knt_ref[part], vnt_ref[part], None,
                              k_ref, v_ref, ko_ref, vo_ref, ot_ref[part], lt_ref[part], group=1, dil=host.dil)
        ot_ref[part] = ot
        lt_ref[part] = lt

    h = h_ref[...]
    g = _dot(h, wg_ref[...])
    u = _dot(h, wu_ref[...])
    o_ref[...] += _dot((_silu(g) * u).astype(BF16), wd_ref[...])

    @pl.when(f == pl.num_programs(1) - 1)
    def _():
        z = alpha * x_ref[...] + res_w * gt_ref[...] * o_ref[...]
        o_ref[...] = _layer_norm(z, lg_ref[...], lb_ref[...])


def _ffn(x, mod, layer, sub, which, wg, wu, wd, ln_g, ln_b, *, tm, tf, alpha, res_w, host=None):
    m, d = x.shape
    f_dim = wg.shape[-1]
    groups = mod.shape[1] if mod.ndim == 4 else 1
    tpg = (m // groups) // tm
    nf = f_dim // tf
    in_specs = ([pl.BlockSpec((tm, d), lambda i, j: (i, 0))]
                + _mod_specs(mod, layer, (3 * sub, 3 * sub + 1, 3 * sub + 2), tpg, d)
                + [pl.BlockSpec((None, None, d, tf), lambda i, j: (layer, which, 0, j)),
                   pl.BlockSpec((None, None, d, tf), lambda i, j: (layer, which, 0, j)),
                   pl.BlockSpec((None, None, tf, d), lambda i, j: (layer, which, j, 0)),
                   _ln_spec(layer, sub, d), _ln_spec(layer, sub, d)])
    args = [x, mod, mod, mod, wg, wu, wd, ln_g, ln_b]
    out_specs = [pl.BlockSpec((tm, d), lambda i, j: (i, 0))]
    out_shape = [jax.ShapeDtypeStruct((m, d), F32)]
    aliases = {}
    static = None
    if host is not None:
        parts, rows, bd = host.qt.shape
        heads, _, w = host.k_view.shape[3:]
        static = _HostStatic(parts, host.first_elem, host.n_elems, host.dil, host.k_acc is not None)
        assert m // tm * nf >= host.n_elems * parts

        def unit_block(i, j):
            unit = jnp.minimum(i * nf + j, host.n_elems * parts - 1)
            return (host.layer, host.first_elem + unit // parts, unit % parts, 0, 0, 0)

        cols = pl.BlockSpec((parts, rows, bd), lambda i, j: (0, 0, 0))
        cache = pl.BlockSpec((None, None, None, heads, HEAD_DIM, w), unit_block)
        in_specs += [cols, cols, cols, cache, cache]
        args += [host.qt, host.knt, host.vnt, host.k_view, host.v_view]
        if host.k_acc is not None:
            aliases = {len(args): 1, len(args) + 1: 2}
            in_specs += [pl.BlockSpec(memory_space=pl.ANY), pl.BlockSpec(memory_space=pl.ANY)]
            args += [host.k_acc, host.v_acc]
        lse = pl.BlockSpec((parts, SUBLANES, bd), lambda i, j: (0, 0, 0))
        out_specs += [cache, cache, cols, lse]
        out_shape += [jax.ShapeDtypeStruct(host.k_view.shape, F32), jax.ShapeDtypeStruct(host.v_view.shape, F32),
                      jax.ShapeDtypeStruct((parts, rows, bd), F32), jax.ShapeDtypeStruct((parts, SUBLANES, bd), F32)]
    out = pl.pallas_call(
        functools.partial(_ffn_kernel, alpha=alpha, res_w=res_w, host=static),
        grid=(m // tm, nf),
        in_specs=in_specs,
        out_specs=out_specs,
        out_shape=out_shape,
        input_output_aliases=aliases,
        scratch_shapes=[pltpu.VMEM((tm, d), BF16)],
        compiler_params=_params("arbitrary", "arbitrary"),
        name="ffn" if host is None else "ffn_host",
    )(*args)
    return out[0] if host is None else out


class _GroupHost(NamedTuple):
    qt: jax.Array
    knt: jax.Array
    vnt: jax.Array
    sink: jax.Array | None
    k_cache: jax.Array
    v_cache: jax.Array
    k_acc: jax.Array | None
    v_acc: jax.Array | None
    dil: int
    first_elem: int
    n_elems: int


class _GroupStatic(NamedTuple):
    group: int
    dil: int
    has_sink: bool
    aliased: bool
    first_unit: int
    n_units: int


def _group_host_operands(hosts, layer, elems, unit_index, n_in, n_out):
    in_specs, args, out_specs, out_shape, aliases, statics = [], [], [], [], {}, []
    for host in hosts:
        _, bd, n_kv, _, w = host.k_cache.shape
        n_q = host.qt.shape[0] // HEAD_DIM
        n_units = host.n_elems // elems
        first_unit = host.first_elem // elems
        statics.append(_GroupStatic(n_q // n_kv, host.dil, host.sink is not None, host.k_acc is not None,
                                    first_unit, n_units))
        full = lambda shape: pl.BlockSpec(shape, lambda *_: (0,) * len(shape))
        cache = pl.BlockSpec((None, elems, n_kv, HEAD_DIM, w),
                             lambda *ids, n=n_units, f=first_unit:
                             (layer, f + jnp.minimum(unit_index(*ids), n - 1), 0, 0, 0))
        in_specs += [full(host.qt.shape), full(host.knt.shape), full(host.vnt.shape)]
        args += [host.qt, host.knt, host.vnt]
        if host.sink is not None:
            in_specs.append(pl.BlockSpec((None, n_q, 1), lambda *_: (layer, 0, 0)))
            args.append(host.sink)
        in_specs += [cache, cache]
        args += [host.k_cache, host.v_cache]
        if host.k_acc is not None:
            aliases[n_in + len(args)] = n_out + len(out_shape)
            aliases[n_in + len(args) + 1] = n_out + len(out_shape) + 1
            in_specs += [pl.BlockSpec(memory_space=pl.ANY), pl.BlockSpec(memory_space=pl.ANY)]
            args += [host.k_acc, host.v_acc]
        out_specs += [cache, cache, full(host.qt.shape), full((n_q, bd))]
        out_shape += [jax.ShapeDtypeStruct(host.k_cache.shape, F32), jax.ShapeDtypeStruct(host.v_cache.shape, F32),
                      jax.ShapeDtypeStruct(host.qt.shape, F32), jax.ShapeDtypeStruct((n_q, bd), F32)]
    return in_specs, args, out_specs, out_shape, aliases, tuple(statics)


def _split_group_host_inputs(refs, pos, statics):
    host_in = []
    for hs in statics:
        n = 5 + hs.has_sink
        host_in.append(refs[pos:pos + n])
        pos += n + (2 if hs.aliased else 0)
    return host_in, pos


def _carry_group_units(statics, host_in, host_out, step, elems, first_step):
    @pl.when(first_step)
    def _():
        for _, _, ot_ref, lt_ref in host_out:
            ot_ref[...] = jnp.zeros_like(ot_ref)
            lt_ref[...] = jnp.zeros_like(lt_ref)

    for hs, ins, (ko_ref, vo_ref, ot_ref, lt_ref) in zip(statics, host_in, host_out):
        unit = hs.first_unit + jnp.minimum(step, hs.n_units - 1)
        qt_ref, knt_ref, vnt_ref = ins[:3]
        sink = ins[3][...] if hs.has_sink else None
        k_ref, v_ref = ins[-2:]
        for e in range(elems):
            ot, lt = _decode_unit(unit * elems + e, qt_ref[...], knt_ref[...], vnt_ref[...], sink,
                                  k_ref.at[e], v_ref.at[e], ko_ref.at[e], vo_ref.at[e], ot_ref[...], lt_ref[...],
                                  group=hs.group, dil=hs.dil)
            ot_ref[...] = ot
            lt_ref[...] = lt


def _proj_kernel(*refs, hosts, elems):
    x_ref, sh_ref, sc_ref, w_ref = refs[:4]
    host_in, pos = _split_group_host_inputs(refs, 4, hosts)
    o_ref = refs[pos]
    pos += 1
    host_out = [refs[pos + 4 * g:pos + 4 * g + 4] for g in range(len(hosts))]
    h_ref = refs[pos + 4 * len(hosts)]
    i = pl.program_id(0)
    j = pl.program_id(1)

    @pl.when(j == 0)
    def _():
        h_ref[...] = (x_ref[...] * (1.0 + sc_ref[...]) + sh_ref[...]).astype(BF16)

    if hosts:
        _carry_group_units(hosts, host_in, host_out, i * pl.num_programs(1) + j, elems, (i == 0) & (j == 0))

    o_ref[...] = _dot(h_ref[...], w_ref[...])


def _proj(x, mod, layer, w_in, *, tm, tn, hosts=(), elems=1):
    m, d = x.shape
    n = w_in.shape[-1]
    groups = mod.shape[1] if mod.ndim == 4 else 1
    tpg = (m // groups) // tm
    nj = n // tn
    in_specs = ([pl.BlockSpec((tm, d), lambda i, j: (i, 0))] + _mod_specs(mod, layer, (3, 4), tpg, d)
                + [pl.BlockSpec((None, d, tn), lambda i, j: (layer, 0, j))])
    args = [x, mod, mod, w_in]
    h_in, h_args, h_out, h_shape, aliases, statics = _group_host_operands(
        hosts, layer, elems, lambda i, j: i * nj + j, len(args), 1)
    assert all(m // tm * nj >= hs.n_units for hs in statics)
    out = pl.pallas_call(
        functools.partial(_proj_kernel, hosts=statics, elems=elems),
        grid=(m // tm, nj),
        in_specs=in_specs + h_in,
        out_specs=[pl.BlockSpec((tm, tn), lambda i, j: (i, j))] + h_out,
        out_shape=[jax.ShapeDtypeStruct((m, n), F32)] + h_shape,
        input_output_aliases=aliases,
        scratch_shapes=[pltpu.VMEM((tm, d), BF16)],
        compiler_params=_params("arbitrary", "arbitrary"),
        name="proj_host" if hosts else "proj_in",
    )(*args, *h_args)
    return out if hosts else out[0]


def _band_kernel(*refs, n_heads, group, dil, has_sink, has_lse, hosts, elems):
    q_ref, kp_ref, kc_ref, vp_ref, vc_ref = refs[:5]
    pos = 5
    sink_ref = None
    if has_sink:
        sink_ref = refs[pos]
        pos += 1
    host_in, pos = _split_group_host_inputs(refs, pos, hosts)
    o_ref = refs[pos]
    lse_ref = refs[pos + 1] if has_lse else None
    pos += 2 if has_lse else 1
    host_out = [refs[pos + 4 * g:pos + 4 * g + 4] for g in range(len(hosts))]
    pos += 4 * len(hosts)
    if hosts:
        step = ((pl.program_id(0) * pl.num_programs(1) + pl.program_id(1)) * pl.num_programs(2)
                + pl.program_id(2))
        _carry_group_units(hosts, host_in, host_out, step, elems, step == 0)

    first_chunk = pl.program_id(1) == 0
    qi = lax.broadcasted_iota(jnp.int32, (SPAN, 2 * SPAN), 0)
    ki = lax.broadcasted_iota(jnp.int32, (SPAN, 2 * SPAN), 1)
    dist = qi + SPAN - ki
    valid = (dist >= 0) & (dist <= SPAN) & ((ki >= SPAN) | jnp.logical_not(first_chunk))

    if dil > 1:
        q_t, kp_t, kc_t, vp_t, vc_t, o_t, lse_t = refs[pos:pos + 7]
        for src, dst in ((q_ref, q_t), (kp_ref, kp_t), (kc_ref, kc_t), (vp_ref, vp_t), (vc_ref, vc_t)):
            for c in range(dst.shape[0]):
                dst[c] = src[:, c * LANES:(c + 1) * LANES]

    def read(ref, tiled, rho):
        if dil == 1:
            return ref[...]
        return jnp.concatenate([tiled[c, pl.ds(rho, SPAN, stride=dil), :] for c in range(tiled.shape[0])], axis=1)

    def write(ref, tiled, rho, val):
        if dil == 1:
            ref[...] = val
        else:
            for c in range(tiled.shape[0]):
                tiled[c, pl.ds(rho, SPAN, stride=dil), :] = val[:, c * LANES:(c + 1) * LANES]

    lane = lax.broadcasted_iota(jnp.int32, (1, LANES), 1)
    half_mask = (lane < HEAD_DIM, lane >= HEAD_DIM)
    ones = jnp.ones((2 * SPAN, LANES), BF16)

    def residue(rho, carry):
        q = read(q_ref, q_t if dil > 1 else None, rho)
        k = jnp.concatenate([read(kp_ref, kp_t if dil > 1 else None, rho),
                             read(kc_ref, kc_t if dil > 1 else None, rho)], axis=0)
        v = jnp.concatenate([read(vp_ref, vp_t if dil > 1 else None, rho),
                             read(vc_ref, vc_t if dil > 1 else None, rho)], axis=0)
        n_qt = n_heads // 2
        o_half = [[None, None] for _ in range(n_qt)]
        l_half = [[None, None] for _ in range(n_qt)]
        for h in range(n_heads // group):
            k_tile = k[:, (h // 2) * LANES:(h // 2 + 1) * LANES]
            v_tile = v[:, (h // 2) * LANES:(h // 2 + 1) * LANES]
            for half in (0, 1):
                hqs = [hq for hq in range(h * group, (h + 1) * group) if hq % 2 == half]
                if not hqs:
                    continue
                aligned = (h % 2) == half
                kh = (k_tile if aligned else pltpu.roll(k_tile, HEAD_DIM, 1)).astype(BF16)
                vh = (v_tile if aligned else pltpu.roll(v_tile, HEAD_DIM, 1)).astype(BF16)
                qg = jnp.concatenate(
                    [jnp.where(half_mask[half], q[:, (hq // 2) * LANES:(hq // 2 + 1) * LANES], 0.0) for hq in hqs],
                    axis=0).astype(BF16)
                s = _dot_nt(qg, kh) * SCALE
                s = jnp.where(jnp.concatenate([valid] * len(hqs), axis=0) if len(hqs) > 1 else valid, s, -jnp.inf)
                m = jnp.max(s, axis=-1, keepdims=True)
                if has_sink:
                    sk = jnp.concatenate([jnp.broadcast_to(sink_ref[:, hq:hq + 1], (SPAN, 1)) for hq in hqs], axis=0)
                    m = jnp.maximum(m, sk)
                e = jnp.exp(s - m).astype(BF16)
                ov = _dot(e, jnp.concatenate([vh, ones], axis=1))
                den = ov[:, LANES:]
                if has_sink:
                    den = den + jnp.exp(sk - m)
                og = ov[:, :LANES] / den
                lg = m + jnp.log(den)
                for i, hq in enumerate(hqs):
                    o_half[hq // 2][half] = og[i * SPAN:(i + 1) * SPAN]
                    l_half[hq // 2][half] = lg[i * SPAN:(i + 1) * SPAN]
        write(o_ref, o_t if dil > 1 else None, rho,
              jnp.concatenate([jnp.where(half_mask[0], a, b) for a, b in o_half], axis=1))
        if has_lse:
            write(lse_ref, lse_t if dil > 1 else None, rho,
                  jnp.concatenate([jnp.where(half_mask[0], a, b) for a, b in l_half], axis=1))
        return carry

    if dil > 1:
        lax.fori_loop(0, dil, residue, 0)
        for c in range(o_t.shape[0]):
            o_ref[:, c * LANES:(c + 1) * LANES] = o_t[c]
            if has_lse:
                lse_ref[:, c * LANES:(c + 1) * LANES] = lse_t[c]
    else:
        residue(0, 0)


def _band_attention(p3, dil, q_off, k_off, v_off, n_kv, group, sink, heads_per_step, hosts=(), elems=1, layer=0):
    b, s, n = p3.shape
    chunk = SPAN * dil
    nsteps = n_kv // heads_per_step
    qw, kw = heads_per_step * group * HEAD_DIM, heads_per_step * HEAD_DIM
    qspec = pl.BlockSpec((None, chunk, qw), lambda bi, c, hb: (bi, c, q_off // qw + hb))

    def kvspec(off, prev):
        if prev:
            return pl.BlockSpec((None, chunk, kw), lambda bi, c, hb: (bi, jnp.maximum(c - 1, 0), off // kw + hb))
        return pl.BlockSpec((None, chunk, kw), lambda bi, c, hb: (bi, c, off // kw + hb))

    in_specs = [qspec, kvspec(k_off, True), kvspec(k_off, False), kvspec(v_off, True), kvspec(v_off, False)]
    args = [p3] * 5
    has_sink = sink is not None
    if has_sink:
        in_specs.append(pl.BlockSpec((None, 1, qw // HEAD_DIM), lambda bi, c, hb: (hb, 0, 0)))
        args.append(sink.reshape(nsteps, 1, qw // HEAD_DIM))
    ospec = pl.BlockSpec((None, chunk, qw), lambda bi, c, hb: (bi, c, hb))
    oshape = jax.ShapeDtypeStruct((b, s, n_kv * group * HEAD_DIM), F32)
    has_lse = not has_sink
    scratch = []
    if dil > 1:
        tiled = lambda width: pltpu.VMEM((width // LANES, chunk, LANES), F32)
        scratch = [tiled(qw)] + [tiled(kw)] * 4 + [tiled(qw)] * 2
    n_chunks = s // chunk
    n_own = 2 if has_lse else 1
    h_in, h_args, h_out, h_shape, aliases, statics = _group_host_operands(
        hosts, layer, elems, lambda bi, c, hb: (bi * n_chunks + c) * nsteps + hb, len(args), n_own)
    assert all(b * n_chunks * nsteps >= hs.n_units for hs in statics)
    out = pl.pallas_call(
        functools.partial(_band_kernel, n_heads=heads_per_step * group, group=group, dil=dil,
                          has_sink=has_sink, has_lse=has_lse, hosts=statics, elems=elems),
        grid=(b, n_chunks, nsteps),
        in_specs=in_specs + h_in,
        out_specs=[ospec] * n_own + h_out,
        out_shape=[oshape] * n_own + h_shape,
        input_output_aliases=aliases,
        scratch_shapes=scratch,
        compiler_params=_params("arbitrary", "arbitrary", "arbitrary"),
        name=f"band_attn_r{dil}" + ("_host" if hosts else ""),
    )(*args, *h_args)
    return out if (has_lse or hosts) else out[0]


def _conv_kernel(ax_ref, ab_ref, ac_ref, axp_ref, acp_ref, w_ref, y_ref, tail_ref, ue_ref, *, tm):
    i = pl.program_id(1)
    u = ac_ref[...] * ax_ref[...]
    halo = acp_ref[...] * axp_ref[...]
    ue_ref[0:SUBLANES, :] = jnp.where(i > 0, halo, 0.0)
    ue_ref[SUBLANES:, :] = u
    u1 = ue_ref[pl.ds(SUBLANES - 1, tm), :]
    u2 = ue_ref[pl.ds(SUBLANES - 2, tm), :]
    y_ref[...] = ab_ref[...] * (w_ref[0:1, :] * u2 + w_ref[1:2, :] * u1 + w_ref[2:3, :] * u)
    tail_ref[...] = u[tm - SUBLANES:, :]


def _conv_prompt(p3, layer, conv_w, a_w, tm):
    b, s, n = p3.shape
    cur = lambda c: pl.BlockSpec((None, tm, a_w), lambda bi, i, c=c: (bi, i, c))
    prev = lambda c: pl.BlockSpec((None, SUBLANES, a_w),
                                  lambda bi, i, c=c: (bi, jnp.maximum(i * (tm // SUBLANES) - 1, 0), c))
    return pl.pallas_call(
        functools.partial(_conv_kernel, tm=tm),
        grid=(b, s // tm),
        in_specs=[cur(0), cur(1), cur(2), prev(0), prev(2),
                  pl.BlockSpec((None, CONV_WIDTH, a_w), lambda bi, i: (layer, 0, 0))],
        out_specs=[pl.BlockSpec((None, tm, a_w), lambda bi, i: (bi, i, 0)),
                   pl.BlockSpec((None, SUBLANES, a_w), lambda bi, i: (bi, 0, 0))],
        out_shape=[jax.ShapeDtypeStruct((b, s, a_w), F32), jax.ShapeDtypeStruct((b, SUBLANES, a_w), F32)],
        scratch_shapes=[pltpu.VMEM((tm + SUBLANES, a_w), F32)],
        compiler_params=_params("arbitrary", "arbitrary"),
        name="short_conv",
    )(p3, p3, p3, p3, p3, conv_w)


def _conv_step_kernel(ax_ref, ab_ref, ac_ref, st_ref, cw_ref, ya_ref, st_out_ref):
    a_w = ax_ref.shape[1]
    u = ac_ref[...] * ax_ref[...]
    s0 = st_ref[:, 0:a_w]
    s1 = st_ref[:, a_w:2 * a_w]
    ya_ref[...] = ab_ref[...] * (cw_ref[0:1, :] * s0 + cw_ref[1:2, :] * s1 + cw_ref[2:3, :] * u)
    st_out_ref[:, 0:a_w] = s1
    st_out_ref[:, a_w:2 * a_w] = u


def _conv_step(ps, layer, conv_state2, conv_w, a_w):
    bd = ps.shape[0]
    col = lambda c: pl.BlockSpec((bd, a_w), lambda i, c=c: (0, c))
    return pl.pallas_call(
        _conv_step_kernel,
        grid=(1,),
        in_specs=[col(0), col(1), col(2),
                  pl.BlockSpec((None, bd, 2 * a_w), lambda i: (layer, 0, 0)),
                  pl.BlockSpec((None, CONV_WIDTH, a_w), lambda i: (layer, 0, 0))],
        out_specs=[pl.BlockSpec((bd, a_w), lambda i: (0, 0)), pl.BlockSpec((bd, 2 * a_w), lambda i: (0, 0))],
        out_shape=[jax.ShapeDtypeStruct((bd, a_w), F32), jax.ShapeDtypeStruct((bd, 2 * a_w), F32)],
        compiler_params=_params("arbitrary"),
        name="conv_step",
    )(ps, ps, ps, conv_state2, conv_w)


def _pick_column(x, onehot):
    return jnp.sum(jnp.where(onehot, x, 0.0), axis=1, keepdims=True)


def _decode_unit(elem, qt, knt, vnt, sink, k_ref, v_ref, ko_ref, vo_ref, ot_old, lt_old, *, group, dil):
    n_kv, _, w = k_ref.shape
    bd = qt.shape[1]
    n_q = n_kv * group
    lane_b = lax.broadcasted_iota(jnp.int32, (1, bd), 1)
    row_b = lax.broadcasted_iota(jnp.int32, (bd, LANES), 0)
    lane_w = lax.broadcasted_iota(jnp.int32, (1, w), 1)
    key_ok = (lane_w % dil) == 0
    last = lane_w == w - 1
    head = lambda x, i: x[i * HEAD_DIM:(i + 1) * HEAD_DIM]

    onehot = lane_b == elem
    q_all = _dot(qt.astype(BF16), (row_b == elem).astype(BF16))
    kn_all = _pick_column(knt, onehot)
    vn_all = _pick_column(vnt, onehot)
    s_rows, sn_rows = [], []
    for h in range(n_kv):
        k = k_ref[h]
        kn = head(kn_all, h)
        ko_ref[h] = jnp.where(last, kn, pltpu.roll(k, w - 1, 1))
        for g in range(group):
            q = head(q_all, h * group + g)
            q_wide = jnp.tile(q, (1, w // LANES)) if w > LANES else q
            s_rows.append(jnp.sum(k * q_wide, axis=0, keepdims=True))
            sn_rows.append(jnp.sum(kn * q[:, 0:1], axis=0, keepdims=True))
    s = jnp.where(key_ok, jnp.concatenate(s_rows, axis=0) * SCALE, -jnp.inf)
    s_new = jnp.concatenate(sn_rows, axis=0) * SCALE
    m = jnp.maximum(jnp.max(s, axis=1, keepdims=True), s_new)
    if sink is not None:
        m = jnp.maximum(m, sink)
    e = jnp.exp(s - m)
    e_new = jnp.exp(s_new - m)
    den = jnp.sum(e, axis=1, keepdims=True) + e_new
    if sink is not None:
        den = den + jnp.exp(sink - m)
    p = e / den
    p_new = e_new / den
    o_cols = []
    for h in range(n_kv):
        v = v_ref[h]
        vn = head(vn_all, h)
        vo_ref[h] = jnp.where(last, vn, pltpu.roll(v, w - 1, 1))
        for g in range(group):
            hq = h * group + g
            o_cols.append(jnp.sum(v * p[hq:hq + 1, :], axis=1, keepdims=True) + p_new[hq:hq + 1, :] * vn)
    lse = m + jnp.log(den)
    if lt_old.shape[0] > n_q:
        lse = jnp.concatenate([lse, jnp.zeros((lt_old.shape[0] - n_q, 1), F32)], axis=0)
    return (jnp.where(onehot, jnp.concatenate(o_cols, axis=0), ot_old), jnp.where(onehot, lse, lt_old))


def _merge_kernel(*refs, alpha, n_lse, d):
    x_ref, gt_ref, ya_ref, yb_ref = refs[:4]
    pos = 4
    o_refs = refs[pos:pos + n_lse]
    l_refs = refs[pos + n_lse:pos + 2 * n_lse]
    pos += 2 * n_lse
    g1_ref, g2_ref, wa_ref, wb_ref, wc_ref, wo_ref, lg_ref, lb_ref, out_ref = refs[pos:]
    ls = [r[...] for r in l_refs]
    mx = functools.reduce(jnp.maximum, ls)
    ws = [jnp.exp(l - mx) for l in ls]
    tot = functools.reduce(lambda a, b: a + b, ws)
    yc = functools.reduce(lambda a, b: a + b, [(w / tot) * o[...] for w, o in zip(ws, o_refs)])
    hw = d // 2
    ga = g1_ref[:, 0:d]
    gb = jnp.concatenate([g1_ref[:, d:d + hw], g2_ref[:, 0:hw]], axis=1)
    gc = g2_ref[:, hw:hw + d]
    m = (jax.nn.sigmoid(ga) * _dot(ya_ref[...].astype(BF16), wa_ref[...])
         + jax.nn.sigmoid(gb) * _dot(yb_ref[...].astype(BF16), wb_ref[...])
         + jax.nn.sigmoid(gc) * _dot(yc.astype(BF16), wc_ref[...]))
    y = _dot(m.astype(BF16), wo_ref[...])
    z = alpha * x_ref[...] + gt_ref[...] * y
    out_ref[...] = _layer_norm(z, lg_ref[...], lb_ref[...])


def _merge(x, mod, layer, ya, yb, outs, lses, proj, g_off, wa, wb, wc, wo, ln_g, ln_b, *, tm, alpha):
    m, d = x.shape
    groups = mod.shape[1] if mod.ndim == 4 else 1
    tpg = (m // groups) // tm
    row = lambda w, c=0: pl.BlockSpec((tm, w), lambda i, c=c: (i, c))
    stacked = lambda a: pl.BlockSpec((None,) + a.shape[1:], lambda i: (layer, 0, 0))
    gw = 3 * d // 2
    in_specs = ([row(d)] + _mod_specs(mod, layer, (5,), tpg, d) + [row(ya.shape[1]), row(yb.shape[1])]
                + [row(o.shape[1]) for o in outs] + [row(l.shape[1]) for l in lses]
                + [row(gw, g_off // gw), row(gw, g_off // gw + 1), stacked(wa), stacked(wb), stacked(wc), stacked(wo),
                   _ln_spec(layer, 1, d), _ln_spec(layer, 1, d)])
    args = [x, mod, ya, yb] + list(outs) + list(lses) + [proj, proj, wa, wb, wc, wo, ln_g, ln_b]
    return pl.pallas_call(
        functools.partial(_merge_kernel, alpha=alpha, n_lse=len(outs), d=d),
        grid=(m // tm,),
        in_specs=in_specs,
        out_specs=row(d),
        out_shape=jax.ShapeDtypeStruct((m, d), F32),
        compiler_params=_params("arbitrary"),
        name="merge",
    )(*args)


def _keys_on_lanes(cache):
    return jnp.transpose(cache, (0, 1, 3, 4, 2))


def _keys_on_rows(cache_t):
    return jnp.transpose(cache_t, (0, 1, 4, 2, 3))


def kernel(x_prompt, x_sample, state_conv, cache_swa_k, cache_swa_v, cache_dil0_k, cache_dil0_v,
           cache_dil1_k, cache_dil1_v, cache_dil2_k, cache_dil2_v, c_prompt, c_sample, w_ada, b_ada,
           ln_g, ln_b, ffn_w_gate, ffn_w_up, ffn_w_down, w_in, conv_w, attn_sink,
           w_br_a, w_br_b, w_br_c, w_out):
    b, s, d = x_prompt.shape
    bd = x_sample.shape[0]
    depth = w_in.shape[0]
    a_w = w_br_a.shape[1]
    swa_q = w_br_b.shape[1]
    dil_w = w_br_c.shape[1]
    swa_kv = cache_swa_k.shape[3] * HEAD_DIM
    n_kv = swa_kv // HEAD_DIM
    swa_group = swa_q // swa_kv
    n_dh = dil_w // HEAD_DIM
    dq = len(DILATIONS) * dil_w
    alpha = (2.0 * depth) ** 0.25
    off_bq = 3 * a_w
    off_bk = off_bq + swa_q
    off_bv = off_bk + swa_kv
    off_cq = off_bv + swa_kv
    off_ck = off_cq + dq
    off_cv = off_ck + dq
    off_g = off_cv + dq
    f_dim = ffn_w_gate.shape[-1]

    tm_ffn = min(512, s)
    tm_proj = min(512, s)
    tf = _largest_tile(f_dim, 512)
    tn_in = _largest_tile(w_in.shape[-1], 1024)
    tn_in_prompt = _largest_tile(w_in.shape[-1], 3072)
    tn_ada = _largest_tile(w_ada.shape[-1], 1024)
    tm_merge = min(256, s)
    tm_conv = min(512, s)
    dil_heads_per_step = tuple(max(LANES // HEAD_DIM, n_dh // max(1, r // 8)) for r in DILATIONS)

    pad = (-(b + bd)) % SUBLANES
    c_rows = jnp.concatenate([c_sample, c_prompt, jnp.zeros((pad, d), F32)], axis=0)
    mod_s, mod_p = _ada(c_rows, bd, b, w_ada, b_ada, tn_ada)

    wg_b, wu_b, wd_b = _to_bf16(ffn_w_gate), _to_bf16(ffn_w_up), _to_bf16(ffn_w_down)
    w_in_b = _to_bf16(w_in)
    wa_b, wb_b, wc_b, wo_b = (_to_bf16(w) for w in (w_br_a, w_br_b, w_br_c, w_out))
    ln_g3 = ln_g.reshape(depth * N_SUB, 1, d)
    ln_b3 = ln_b.reshape(depth * N_SUB, 1, d)
    sink3 = attn_sink.reshape(depth, -1, 1)

    dec_groups = [(_keys_on_lanes(cache_swa_k), _keys_on_lanes(cache_swa_v), 1, (("proj", 0, bd),)),
                  (_keys_on_lanes(cache_dil0_k), _keys_on_lanes(cache_dil0_v), DILATIONS[0], (("proj", 0, bd),)),
                  (_keys_on_lanes(cache_dil1_k), _keys_on_lanes(cache_dil1_v), DILATIONS[1], (("swa", 0, bd),))]
    dec_acc = [(None, None)] * len(dec_groups)
    host_parts = 2
    host_heads = n_dh // host_parts
    host_shape = (depth, bd, host_parts, host_heads, HEAD_DIM, cache_dil2_k.shape[2])
    host_k = _keys_on_lanes(cache_dil2_k).reshape(host_shape)
    host_v = _keys_on_lanes(cache_dil2_v).reshape(host_shape)
    host_acc = (None, None)
    host_gi = len(DILATIONS) - 1
    conv2 = state_conv.reshape(depth, bd, 2 * a_w)

    xp = x_prompt.reshape(b * s, d)
    xs = x_sample.reshape(bd, d)
    new_p = [[] for _ in range(9)]
    conv_s = []
    for l in range(depth):
        ffn = lambda x, mod, sub, which, tm, host=None: _ffn(x, mod, l, sub, which, wg_b, wu_b, wd_b, ln_g3, ln_b3,
                                                             tm=tm, tf=tf, alpha=alpha, res_w=0.5, host=host)
        xs = ffn(xs, mod_s, 0, 0, bd)
        ps = _proj(xs, mod_s, l, w_in_b, tm=bd, tn=tn_in)
        ya_s, st_s = _conv_step(ps, l, conv2, conv_w, a_w)
        conv_s.append(st_s.reshape(bd, CONV_WIDTH - 1, a_w))
        pst = ps[:, off_bq:off_g].T
        col = lambda off, width: pst[off - off_bq:off - off_bq + width]
        outs_s, lses_s = [0.0] * len(dec_groups), [0.0] * len(dec_groups)

        def carry(where):
            hosts = []
            for gidx, (kc, vc, dil, plan) in enumerate(dec_groups):
                for place, first, count in plan:
                    if place != where:
                        continue
                    if gidx == 0:
                        qt, knt, vnt, sink = col(off_bq, swa_q), col(off_bk, swa_kv), col(off_bv, swa_kv), sink3
                    else:
                        gi = gidx - 1
                        qt, knt, vnt, sink = (col(off_cq + gi * dil_w, dil_w), col(off_ck + gi * dil_w, dil_w),
                                              col(off_cv + gi * dil_w, dil_w), None)
                    hosts.append((gidx, _GroupHost(qt, knt, vnt, sink, kc, vc, *dec_acc[gidx], dil, first, count)))
            return hosts

        def collect(hosts, hosted):
            for n, (gidx, _) in enumerate(hosts):
                ko, vo, ot, lt = hosted[4 * n:4 * n + 4]
                dec_acc[gidx] = (ko, vo)
                outs_s[gidx] = outs_s[gidx] + ot.T
                lses_s[gidx] = lses_s[gidx] + jnp.repeat(lt.T, HEAD_DIM, axis=1)
        host_cols = [col(off + host_gi * dil_w, dil_w).reshape(host_parts, host_heads * HEAD_DIM, bd)
                     for off in (off_cq, off_ck, off_cv)]
        make_host = lambda first: _DecodeHost(*host_cols, host_k, host_v, *host_acc, l, first, bd // 2,
                                              DILATIONS[host_gi])
        xp, hk, hv, ot_a, lt_a = ffn(xp, mod_p, 0, 0, tm_ffn, make_host(0))
        host_acc = (hk, hv)
        hosts = carry("proj")
        pp, *hosted = _proj(xp, mod_p, l, w_in_b, tm=tm_proj, tn=tn_in_prompt,
                            hosts=tuple(h for _, h in hosts), elems=2)
        collect(hosts, hosted)
        p3 = pp.reshape(b, s, -1)
        ya_p, u_tail = _conv_prompt(p3, l, conv_w, a_w, tm_conv)
        hosts = carry("swa")
        yb_p, *hosted = _band_attention(p3, 1, off_bq, off_bk, off_bv, n_kv, swa_group, attn_sink[l], n_kv,
                                        hosts=tuple(h for _, h in hosts), elems=2, layer=l)
        collect(hosts, hosted)
        outs, lses = [], []
        for gi, r in enumerate(DILATIONS):
            og, lg = _band_attention(p3, r, off_cq + gi * dil_w, off_ck + gi * dil_w, off_cv + gi * dil_w,
                                     n_dh, 1, None, dil_heads_per_step[gi])
            outs.append(og.reshape(b * s, dil_w))
            lses.append(lg.reshape(b * s, dil_w))
        xp = _merge(xp, mod_p, l, ya_p.reshape(b * s, a_w), yb_p.reshape(b * s, swa_q), outs, lses, pp, off_g,
                    wa_b, wb_b, wc_b, wo_b, ln_g3, ln_b3, tm=tm_merge, alpha=alpha)
        xp, hk, hv, ot_b, lt_b = ffn(xp, mod_p, 2, 1, tm_ffn, make_host(bd // 2))
        host_acc = (hk, hv)
        outs_s.append((ot_a + ot_b).reshape(dil_w, bd).T)
        lt_h = (lt_a + lt_b)[:, :host_heads].reshape(n_dh, bd)
        lses_s.append(jnp.repeat(lt_h.T, HEAD_DIM, axis=1))
        xs = _merge(xs, mod_s, l, ya_s, outs_s[0], outs_s[1:], lses_s[1:], ps, off_g,
                    wa_b, wb_b, wc_b, wo_b, ln_g3, ln_b3, tm=bd, alpha=alpha)
        xs = ffn(xs, mod_s, 2, 1, bd)

        rows = min(SPAN, s)
        st_p = [u_tail[:, SUBLANES - (CONV_WIDTH - 1):],
                p3[:, s - rows:, off_bk:off_bk + swa_kv].reshape(b, rows, n_kv, HEAD_DIM),
                p3[:, s - rows:, off_bv:off_bv + swa_kv].reshape(b, rows, n_kv, HEAD_DIM)]
        for gi, r in enumerate(DILATIONS):
            rows = min(SPAN * r, s)
            for off in (off_ck, off_cv):
                c0 = off + gi * dil_w
                st_p.append(p3[:, s - rows:, c0:c0 + dil_w].reshape(b, rows, n_dh, HEAD_DIM))
        for acc, v in zip(new_p, st_p):
            acc.append(v)

    P = [jnp.stack(a, 0) for a in new_p]
    S = [jnp.stack(conv_s, 0)]
    for ko, vo in dec_acc:
        S += [_keys_on_rows(ko), _keys_on_rows(vo)]
    S += [_keys_on_rows(c.reshape(depth, bd, n_dh, HEAD_DIM, -1)) for c in host_acc]
    out = [xp.reshape(b, s, d), xs.reshape(bd, 1, d)]
    for pa, sa in zip(P, S):
        out += [pa, sa]
    return tuple(out)
```

```python
import functools
from typing import NamedTuple

import jax
import jax.numpy as jnp
from jax import lax
from jax.experimental import pallas as pl
from jax.experimental.pallas import tpu as pltpu

HEAD_DIM = 64
SPAN = 128
CONV_WIDTH = 3
N_SUB = 3
DILATIONS = (1, 4, 16)
LN_EPS = 1e-5
SCALE = HEAD_DIM ** -0.5
SUBLANES = 8
LANES = 128
VMEM_LIMIT_BYTES = 58 * 1024 * 1024

F32 = jnp.float32
BF16 = jnp.bfloat16


def _params(*sem):
    return pltpu.CompilerParams(dimension_semantics=sem, vmem_limit_bytes=VMEM_LIMIT_BYTES)


def _largest_tile(n, cap):
    return max(t for t in range(128, cap + 1, 128) if n % t == 0)


def _dot(a, b):
    return jnp.dot(a, b, preferred_element_type=F32)


def _dot_nt(a, b):
    return lax.dot_general(a, b, (((1,), (1,)), ((), ())), preferred_element_type=F32)


def _layer_norm(z, g, b):
    mu = jnp.mean(z, axis=-1, keepdims=True)
    zc = z - mu
    var = jnp.mean(zc * zc, axis=-1, keepdims=True)
    return zc * lax.rsqrt(var + LN_EPS) * g + b


def _silu(x):
    return x * jax.nn.sigmoid(x)


def _mod_specs(mod, layer, cols, tiles_per_group, d):
    if mod.ndim == 4:
        return [pl.BlockSpec((None, None, 1, d), lambda i, *_, c=c: (layer, i // tiles_per_group, 0, c))
                for c in cols]
    return [pl.BlockSpec((None, mod.shape[1], d), lambda i, *_, c=c: (layer, 0, c)) for c in cols]


def _ln_spec(layer, sub, d):
    return pl.BlockSpec((None, 1, d), lambda *_: (layer * N_SUB + sub, 0, 0))


CAST_BLOCK_BYTES = 8 * 1024 * 1024


def _cast_kernel(x_ref, o_ref):
    o_ref[...] = x_ref[...].astype(BF16)


def _to_bf16(w):
    r, c = w.shape[-2:]
    w3 = w.reshape((-1, r, c))
    tr = max(t for t in range(16, r + 1, 16) if r % t == 0 and t * c * 4 <= CAST_BLOCK_BYTES)
    out = pl.pallas_call(
        _cast_kernel,
        grid=(w3.shape[0], r // tr),
        in_specs=[pl.BlockSpec((None, tr, c), lambda a, i: (a, i, 0))],
        out_specs=pl.BlockSpec((None, tr, c), lambda a, i: (a, i, 0)),
        out_shape=jax.ShapeDtypeStruct(w3.shape, BF16),
        compiler_params=_params("arbitrary", "arbitrary"),
        name="to_bf16",
    )(w3)
    return out.reshape(w.shape)


def _ada_kernel(c_ref, w_ref, b_ref, os_ref, op_ref, *, bd, b):
    a = _silu(c_ref[...]).astype(BF16)
    y = _dot(a, w_ref[...].astype(BF16)) + b_ref[...]
    os_ref[...] = y[0:bd]
    for i in range(b):
        op_ref[i] = y[bd + i:bd + i + 1]


def _ada(c_rows, bd, b, w_ada, b_ada, tn):
    depth, d, n = w_ada.shape
    r = c_rows.shape[0]
    return pl.pallas_call(
        functools.partial(_ada_kernel, bd=bd, b=b),
        grid=(depth, n // tn),
        in_specs=[pl.BlockSpec((r, d), lambda l, j: (0, 0)),
                  pl.BlockSpec((None, d, tn), lambda l, j: (l, 0, j)),
                  pl.BlockSpec((None, 1, tn), lambda l, j: (l, 0, j))],
        out_specs=[pl.BlockSpec((None, bd, tn), lambda l, j: (l, 0, j)),
                   pl.BlockSpec((None, b, 1, tn), lambda l, j: (l, 0, 0, j))],
        out_shape=[jax.ShapeDtypeStruct((depth, bd, n), F32), jax.ShapeDtypeStruct((depth, b, 1, n), F32)],
        compiler_params=_params("arbitrary", "arbitrary"),
        name="adaln",
    )(c_rows, w_ada, b_ada.reshape(depth, 1, n))


class _DecodeHost(NamedTuple):
    qt: jax.Array
    knt: jax.Array
    vnt: jax.Array
    k_view: jax.Array
    v_view: jax.Array
    k_acc: jax.Array | None
    v_acc: jax.Array | None
    layer: int
    first_elem: int
    n_elems: int
    dil: int


class _HostStatic(NamedTuple):
    parts: int
    first_elem: int
    n_elems: int
    dil: int
    aliased: bool


def _ffn_kernel(*refs, alpha, res_w, host):
    x_ref, sh_ref, sc_ref, gt_ref, wg_ref, wu_ref, wd_ref, lg_ref, lb_ref = refs[:9]
    pos = 9
    if host is not None:
        qt_ref, knt_ref, vnt_ref, k_ref, v_ref = refs[pos:pos + 5]
        pos += 7 if host.aliased else 5
    o_ref = refs[pos]
    pos += 1
    if host is not None:
        ko_ref, vo_ref, ot_ref, lt_ref = refs[pos:pos + 4]
        pos += 4
    h_ref, acc_ref = refs[pos:pos + 2]
    p, f, t = pl.program_id(0), pl.program_id(1), pl.program_id(2)

    @pl.when(f == 0)
    def _():
        h_ref[t] = (x_ref[...] * (1.0 + sc_ref[...]) + sh_ref[...]).astype(BF16)
        acc_ref[t] = jnp.zeros(acc_ref.shape[1:], F32)

    if host is not None:
        step = (p * pl.num_programs(1) + f) * pl.num_programs(2) + t

        @pl.when(step == 0)
        def _():
            ot_ref[...] = jnp.zeros_like(ot_ref)
            lt_ref[...] = jnp.zeros_like(lt_ref)

        unit = jnp.minimum(step, host.n_elems * host.parts - 1)
        part = unit % host.parts
        ot, lt = _decode_unit(host.first_elem + unit // host.parts, qt_ref[part], knt_ref[part], vnt_ref[part], None,
                              k_ref, v_ref, ko_ref, vo_ref, ot_ref[part], lt_ref[part], group=1, dil=host.dil)
        ot_ref[part] = ot
        lt_ref[part] = lt

    h = h_ref[t]
    g = _dot(h, wg_ref[...])
    u = _dot(h, wu_ref[...])
    acc_ref[t] += _dot((_silu(g) * u).astype(BF16), wd_ref[...])

    @pl.when(f == pl.num_programs(1) - 1)
    def _():
        z = alpha * x_ref[...] + res_w * gt_ref[...] * acc_ref[t]
        o_ref[...] = _layer_norm(z, lg_ref[...], lb_ref[...])


def _ffn(x, mod, layer, sub, which, wg, wu, wd, ln_g, ln_b, *, tm, tf, pair, alpha, res_w, host=None):
    m, d = x.shape
    f_dim = wg.shape[-1]
    groups = mod.shape[1] if mod.ndim == 4 else 1
    nf = f_dim // tf
    tpg = (m // groups) // (tm * pair)
    ends = lambda f: (f == 0) | (f == nf - 1)
    in_specs = ([pl.BlockSpec((tm, d), lambda p, f, t: (pair * p + jnp.where(ends(f), t, 0), 0))]
                + _mod_specs(mod, layer, (3 * sub, 3 * sub + 1, 3 * sub + 2), tpg, d)
                + [pl.BlockSpec((None, None, d, tf), lambda p, f, t: (layer, which, 0, f)),
                   pl.BlockSpec((None, None, d, tf), lambda p, f, t: (layer, which, 0, f)),
                   pl.BlockSpec((None, None, tf, d), lambda p, f, t: (layer, which, f, 0)),
                   _ln_spec(layer, sub, d), _ln_spec(layer, sub, d)])
    args = [x, mod, mod, mod, wg, wu, wd, ln_g, ln_b]
    out_specs = [pl.BlockSpec((tm, d), lambda p, f, t: (pair * p + jnp.where(f == nf - 1, t, 0), 0))]
    out_shape = [jax.ShapeDtypeStruct((m, d), F32)]
    aliases = {}
    static = None
    if host is not None:
        parts, rows, bd = host.qt.shape
        heads, _, w = host.k_view.shape[3:]
        static = _HostStatic(parts, host.first_elem, host.n_elems, host.dil, host.k_acc is not None)
        assert m // tm * nf >= host.n_elems * parts

        def unit_block(p, f, t):
            unit = jnp.minimum((p * nf + f) * pair + t, host.n_elems * parts - 1)
            return (host.layer, host.first_elem + unit // parts, unit % parts, 0, 0, 0)

        cols = pl.BlockSpec((parts, rows, bd), lambda p, f, t: (0, 0, 0))
        cache = pl.BlockSpec((None, None, None, heads, HEAD_DIM, w), unit_block)
        in_specs += [cols, cols, cols, cache, cache]
        args += [host.qt, host.knt, host.vnt, host.k_view, host.v_view]
        if host.k_acc is not None:
            aliases = {len(args): 1, len(args) + 1: 2}
            in_specs += [pl.BlockSpec(memory_space=pl.ANY), pl.BlockSpec(memory_space=pl.ANY)]
            args += [host.k_acc, host.v_acc]
        lse = pl.BlockSpec((parts, SUBLANES, bd), lambda p, f, t: (0, 0, 0))
        out_specs += [cache, cache, cols, lse]
        out_shape += [jax.ShapeDtypeStruct(host.k_view.shape, F32), jax.ShapeDtypeStruct(host.v_view.shape, F32),
                      jax.ShapeDtypeStruct((parts, rows, bd), F32), jax.ShapeDtypeStruct((parts, SUBLANES, bd), F32)]
    out = pl.pallas_call(
        functools.partial(_ffn_kernel, alpha=alpha, res_w=res_w, host=static),
        grid=(m // (tm * pair), nf, pair),
        in_specs=in_specs,
        out_specs=out_specs,
        out_shape=out_shape,
        input_output_aliases=aliases,
        scratch_shapes=[pltpu.VMEM((pair, tm, d), BF16), pltpu.VMEM((pair, tm, d), F32)],
        compiler_params=_params("arbitrary", "arbitrary", "arbitrary"),
        name="ffn" if host is None else "ffn_host",
    )(*args)
    return out[0] if host is None else out


class _GroupHost(NamedTuple):
    qt: jax.Array
    knt: jax.Array
    vnt: jax.Array
    sink: jax.Array | None
    k_cache: jax.Array
    v_cache: jax.Array
    k_acc: jax.Array | None
    v_acc: jax.Array | None
    dil: int
    first_elem: int
    n_elems: int


class _GroupStatic(NamedTuple):
    group: int
    dil: int
    has_sink: bool
    aliased: bool
    first_unit: int
    n_units: int


def _group_host_operands(hosts, layer, elems, unit_index, n_in, n_out):
    in_specs, args, out_specs, out_shape, aliases, statics = [], [], [], [], {}, []
    for host in hosts:
        _, bd, n_kv, _, w = host.k_cache.shape
        n_q = host.qt.shape[0] // HEAD_DIM
        n_units = host.n_elems // elems
        first_unit = host.first_elem // elems
        statics.append(_GroupStatic(n_q // n_kv, host.dil, host.sink is not None, host.k_acc is not None,
                                    first_unit, n_units))
        full = lambda shape: pl.BlockSpec(shape, lambda *_: (0,) * len(shape))
        cache = pl.BlockSpec((None, elems, n_kv, HEAD_DIM, w),
                             lambda *ids, n=n_units, f=first_unit:
                             (layer, f + jnp.minimum(unit_index(*ids), n - 1), 0, 0, 0))
        in_specs += [full(host.qt.shape), full(host.knt.shape), full(host.vnt.shape)]
        args += [host.qt, host.knt, host.vnt]
        if host.sink is not None:
            in_specs.append(pl.BlockSpec((None, n_q, 1), lambda *_: (layer, 0, 0)))
            args.append(host.sink)
        in_specs += [cache, cache]
        args += [host.k_cache, host.v_cache]
        if host.k_acc is not None:
            aliases[n_in + len(args)] = n_out + len(out_shape)
            aliases[n_in + len(args) + 1] = n_out + len(out_shape) + 1
            in_specs += [pl.BlockSpec(memory_space=pl.ANY), pl.BlockSpec(memory_space=pl.ANY)]
            args += [host.k_acc, host.v_acc]
        out_specs += [cache, cache, full(host.qt.shape), full((n_q, bd))]
        out_shape += [jax.ShapeDtypeStruct(host.k_cache.shape, F32), jax.ShapeDtypeStruct(host.v_cache.shape, F32),
                      jax.ShapeDtypeStruct(host.qt.shape, F32), jax.ShapeDtypeStruct((n_q, bd), F32)]
    return in_specs, args, out_specs, out_shape, aliases, tuple(statics)


def _split_group_host_inputs(refs, pos, statics):
    host_in = []
    for hs in statics:
        n = 5 + hs.has_sink
        host_in.append(refs[pos:pos + n])
        pos += n + (2 if hs.aliased else 0)
    return host_in, pos


def _carry_group_units(statics, host_in, host_out, step, elems, first_step):
    @pl.when(first_step)
    def _():
        for _, _, ot_ref, lt_ref in host_out:
            ot_ref[...] = jnp.zeros_like(ot_ref)
            lt_ref[...] = jnp.zeros_like(lt_ref)

    for hs, ins, (ko_ref, vo_ref, ot_ref, lt_ref) in zip(statics, host_in, host_out):
        unit = hs.first_unit + jnp.minimum(step, hs.n_units - 1)
        qt_ref, knt_ref, vnt_ref = ins[:3]
        sink = ins[3][...] if hs.has_sink else None
        k_ref, v_ref = ins[-2:]
        for e in range(elems):
            ot, lt = _decode_unit(unit * elems + e, qt_ref[...], knt_ref[...], vnt_ref[...], sink,
                                  k_ref.at[e], v_ref.at[e], ko_ref.at[e], vo_ref.at[e], ot_ref[...], lt_ref[...],
                                  group=hs.group, dil=hs.dil)
            ot_ref[...] = ot
            lt_ref[...] = lt


def _proj_kernel(*refs, hosts, elems):
    x_ref, sh_ref, sc_ref, w_ref = refs[:4]
    host_in, pos = _split_group_host_inputs(refs, 4, hosts)
    o_ref = refs[pos]
    pos += 1
    host_out = [refs[pos + 4 * g:pos + 4 * g + 4] for g in range(len(hosts))]
    h_ref = refs[pos + 4 * len(hosts)]
    i = pl.program_id(0)
    j = pl.program_id(1)

    @pl.when(j == 0)
    def _():
        h_ref[...] = (x_ref[...] * (1.0 + sc_ref[...]) + sh_ref[...]).astype(BF16)

    if hosts:
        _carry_group_units(hosts, host_in, host_out, i * pl.num_programs(1) + j, elems, (i == 0) & (j == 0))

    o_ref[...] = _dot(h_ref[...], w_ref[...])


def _proj(x, mod, layer, w_in, *, tm, tn, hosts=(), elems=1):
    m, d = x.shape
    n = w_in.shape[-1]
    groups = mod.shape[1] if mod.ndim == 4 else 1
    tpg = (m // groups) // tm
    nj = n // tn
    in_specs = ([pl.BlockSpec((tm, d), lambda i, j: (i, 0))] + _mod_specs(mod, layer, (3, 4), tpg, d)
                + [pl.BlockSpec((None, d, tn), lambda i, j: (layer, 0, j))])
    args = [x, mod, mod, w_in]
    h_in, h_args, h_out, h_shape, aliases, statics = _group_host_operands(
        hosts, layer, elems, lambda i, j: i * nj + j, len(args), 1)
    assert all(m // tm * nj >= hs.n_units for hs in statics)
    out = pl.pallas_call(
        functools.partial(_proj_kernel, hosts=statics, elems=elems),
        grid=(m // tm, nj),
        in_specs=in_specs + h_in,
        out_specs=[pl.BlockSpec((tm, tn), lambda i, j: (i, j))] + h_out,
        out_shape=[jax.ShapeDtypeStruct((m, n), F32)] + h_shape,
        input_output_aliases=aliases,
        scratch_shapes=[pltpu.VMEM((tm, d), BF16)],
        compiler_params=_params("arbitrary", "arbitrary"),
        name="proj_host" if hosts else "proj_in",
    )(*args, *h_args)
    return out if hosts else out[0]


def _band_kernel(*refs, n_heads, group, dil, has_sink, has_lse, hosts, elems):
    q_ref, kp_ref, kc_ref, vp_ref, vc_ref = refs[:5]
    pos = 5
    sink_ref = None
    if has_sink:
        sink_ref = refs[pos]
        pos += 1
    host_in, pos = _split_group_host_inputs(refs, pos, hosts)
    o_ref = refs[pos]
    lse_ref = refs[pos + 1] if has_lse else None
    pos += 2 if has_lse else 1
    host_out = [refs[pos + 4 * g:pos + 4 * g + 4] for g in range(len(hosts))]
    pos += 4 * len(hosts)
    if hosts:
        step = ((pl.program_id(0) * pl.num_programs(1) + pl.program_id(1)) * pl.num_programs(2)
                + pl.program_id(2))
        _carry_group_units(hosts, host_in, host_out, step, elems, step == 0)

    first_chunk = pl.program_id(1) == 0
    qi = lax.broadcasted_iota(jnp.int32, (SPAN, 2 * SPAN), 0)
    ki = lax.broadcasted_iota(jnp.int32, (SPAN, 2 * SPAN), 1)
    dist = qi + SPAN - ki
    valid = (dist >= 0) & (dist <= SPAN) & ((ki >= SPAN) | jnp.logical_not(first_chunk))

    if dil > 1:
        q_t, kp_t, kc_t, vp_t, vc_t, o_t, lse_t = refs[pos:pos + 7]
        for src, dst in ((q_ref, q_t), (kp_ref, kp_t), (kc_ref, kc_t), (vp_ref, vp_t), (vc_ref, vc_t)):
            for c in range(dst.shape[0]):
                dst[c] = src[:, c * LANES:(c + 1) * LANES]

    def read(ref, tiled, rho):
        if dil == 1:
            return ref[...]
        return jnp.concatenate([tiled[c, pl.ds(rho, SPAN, stride=dil), :] for c in range(tiled.shape[0])], axis=1)

    def write(ref, tiled, rho, val):
        if dil == 1:
            ref[...] = val
        else:
            for c in range(tiled.shape[0]):
                tiled[c, pl.ds(rho, SPAN, stride=dil), :] = val[:, c * LANES:(c + 1) * LANES]

    lane = lax.broadcasted_iota(jnp.int32, (1, LANES), 1)
    half_mask = (lane < HEAD_DIM, lane >= HEAD_DIM)
    ones = jnp.ones((2 * SPAN, LANES), BF16)

    def residue(rho, carry):
        q = read(q_ref, q_t if dil > 1 else None, rho)
        k = jnp.concatenate([read(kp_ref, kp_t if dil > 1 else None, rho),
                             read(kc_ref, kc_t if dil > 1 else None, rho)], axis=0)
        v = jnp.concatenate([read(vp_ref, vp_t if dil > 1 else None, rho),
                             read(vc_ref, vc_t if dil > 1 else None, rho)], axis=0)
        n_qt = n_heads // 2
        o_half = [[None, None] for _ in range(n_qt)]
        l_half = [[None, None] for _ in range(n_qt)]
        for h in range(n_heads // group):
            k_tile = k[:, (h // 2) * LANES:(h // 2 + 1) * LANES]
            v_tile = v[:, (h // 2) * LANES:(h // 2 + 1) * LANES]
            for half in (0, 1):
                hqs = [hq for hq in range(h * group, (h + 1) * group) if hq % 2 == half]
                if not hqs:
                    continue
                aligned = (h % 2) == half
                kh = (k_tile if aligned else pltpu.roll(k_tile, HEAD_DIM, 1)).astype(BF16)
                vh = (v_tile if aligned else pltpu.roll(v_tile, HEAD_DIM, 1)).astype(BF16)
                qg = jnp.concatenate(
                    [jnp.where(half_mask[half], q[:, (hq // 2) * LANES:(hq // 2 + 1) * LANES], 0.0) for hq in hqs],
                    axis=0).astype(BF16)
                s = _dot_nt(qg, kh) * SCALE
                s = jnp.where(jnp.concatenate([valid] * len(hqs), axis=0) if len(hqs) > 1 else valid, s, -jnp.inf)
                m = jnp.max(s, axis=-1, keepdims=True)
                if has_sink:
                    sk = jnp.concatenate([jnp.broadcast_to(sink_ref[:, hq:hq + 1], (SPAN, 1)) for hq in hqs], axis=0)
                    m = jnp.maximum(m, sk)
                e = jnp.exp(s - m).astype(BF16)
                ov = _dot(e, jnp.concatenate([vh, ones], axis=1))
                den = ov[:, LANES:]
                if has_sink:
                    den = den + jnp.exp(sk - m)
                og = ov[:, :LANES] / den
                lg = m + jnp.log(den)
                for i, hq in enumerate(hqs):
                    o_half[hq // 2][half] = og[i * SPAN:(i + 1) * SPAN]
                    l_half[hq // 2][half] = lg[i * SPAN:(i + 1) * SPAN]
        write(o_ref, o_t if dil > 1 else None, rho,
              jnp.concatenate([jnp.where(half_mask[0], a, b) for a, b in o_half], axis=1))
        if has_lse:
            write(lse_ref, lse_t if dil > 1 else None, rho,
                  jnp.concatenate([jnp.where(half_mask[0], a, b) for a, b in l_half], axis=1))
        return carry

    if dil > 1:
        lax.fori_loop(0, dil, residue, 0)
        for c in range(o_t.shape[0]):
            o_ref[:, c * LANES:(c + 1) * LANES] = o_t[c]
            if has_lse:
                lse_ref[:, c * LANES:(c + 1) * LANES] = lse_t[c]
    else:
        residue(0, 0)


def _band_attention(p3, dil, q_off, k_off, v_off, n_kv, group, sink, heads_per_step, hosts=(), elems=1, layer=0):
    b, s, n = p3.shape
    chunk = SPAN * dil
    nsteps = n_kv // heads_per_step
    qw, kw = heads_per_step * group * HEAD_DIM, heads_per_step * HEAD_DIM
    qspec = pl.BlockSpec((None, chunk, qw), lambda bi, c, hb: (bi, c, q_off // qw + hb))

    def kvspec(off, prev):
        if prev:
            return pl.BlockSpec((None, chunk, kw), lambda bi, c, hb: (bi, jnp.maximum(c - 1, 0), off // kw + hb))
        return pl.BlockSpec((None, chunk, kw), lambda bi, c, hb: (bi, c, off // kw + hb))

    in_specs = [qspec, kvspec(k_off, True), kvspec(k_off, False), kvspec(v_off, True), kvspec(v_off, False)]
    args = [p3] * 5
    has_sink = sink is not None
    if has_sink:
        in_specs.append(pl.BlockSpec((None, 1, qw // HEAD_DIM), lambda bi, c, hb: (hb, 0, 0)))
        args.append(sink.reshape(nsteps, 1, qw // HEAD_DIM))
    ospec = pl.BlockSpec((None, chunk, qw), lambda bi, c, hb: (bi, c, hb))
    oshape = jax.ShapeDtypeStruct((b, s, n_kv * group * HEAD_DIM), F32)
    has_lse = not has_sink
    scratch = []
    if dil > 1:
        tiled = lambda width: pltpu.VMEM((width // LANES, chunk, LANES), F32)
        scratch = [tiled(qw)] + [tiled(kw)] * 4 + [tiled(qw)] * 2
    n_chunks = s // chunk
    n_own = 2 if has_lse else 1
    h_in, h_args, h_out, h_shape, aliases, statics = _group_host_operands(
        hosts, layer, elems, lambda bi, c, hb: (bi * n_chunks + c) * nsteps + hb, len(args), n_own)
    assert all(b * n_chunks * nsteps >= hs.n_units for hs in statics)
    out = pl.pallas_call(
        functools.partial(_band_kernel, n_heads=heads_per_step * group, group=group, dil=dil,
                          has_sink=has_sink, has_lse=has_lse, hosts=statics, elems=elems),
        grid=(b, n_chunks, nsteps),
        in_specs=in_specs + h_in,
        out_specs=[ospec] * n_own + h_out,
        out_shape=[oshape] * n_own + h_shape,
        input_output_aliases=aliases,
        scratch_shapes=scratch,
        compiler_params=_params("arbitrary", "arbitrary", "arbitrary"),
        name=f"band_attn_r{dil}" + ("_host" if hosts else ""),
    )(*args, *h_args)
    return out if (has_lse or hosts) else out[0]


def _conv_kernel(ax_ref, ab_ref, ac_ref, axp_ref, acp_ref, w_ref, y_ref, tail_ref, ue_ref, *, tm):
    i = pl.program_id(1)
    u = ac_ref[...] * ax_ref[...]
    halo = acp_ref[...] * axp_ref[...]
    ue_ref[0:SUBLANES, :] = jnp.where(i > 0, halo, 0.0)
    ue_ref[SUBLANES:, :] = u
    u1 = ue_ref[pl.ds(SUBLANES - 1, tm), :]
    u2 = ue_ref[pl.ds(SUBLANES - 2, tm), :]
    y_ref[...] = ab_ref[...] * (w_ref[0:1, :] * u2 + w_ref[1:2, :] * u1 + w_ref[2:3, :] * u)
    tail_ref[...] = u[tm - SUBLANES:, :]


def _conv_prompt(p3, layer, conv_w, a_w, tm):
    b, s, n = p3.shape
    cur = lambda c: pl.BlockSpec((None, tm, a_w), lambda bi, i, c=c: (bi, i, c))
    prev = lambda c: pl.BlockSpec((None, SUBLANES, a_w),
                                  lambda bi, i, c=c: (bi, jnp.maximum(i * (tm // SUBLANES) - 1, 0), c))
    return pl.pallas_call(
        functools.partial(_conv_kernel, tm=tm),
        grid=(b, s // tm),
        in_specs=[cur(0), cur(1), cur(2), prev(0), prev(2),
                  pl.BlockSpec((None, CONV_WIDTH, a_w), lambda bi, i: (layer, 0, 0))],
        out_specs=[pl.BlockSpec((None, tm, a_w), lambda bi, i: (bi, i, 0)),
                   pl.BlockSpec((None, SUBLANES, a_w), lambda bi, i: (bi, 0, 0))],
        out_shape=[jax.ShapeDtypeStruct((b, s, a_w), F32), jax.ShapeDtypeStruct((b, SUBLANES, a_w), F32)],
        scratch_shapes=[pltpu.VMEM((tm + SUBLANES, a_w), F32)],
        compiler_params=_params("arbitrary", "arbitrary"),
        name="short_conv",
    )(p3, p3, p3, p3, p3, conv_w)


def _conv_step_kernel(ax_ref, ab_ref, ac_ref, st_ref, cw_ref, ya_ref, st_out_ref):
    a_w = ax_ref.shape[1]
    u = ac_ref[...] * ax_ref[...]
    s0 = st_ref[:, 0:a_w]
    s1 = st_ref[:, a_w:2 * a_w]
    ya_ref[...] = ab_ref[...] * (cw_ref[0:1, :] * s0 + cw_ref[1:2, :] * s1 + cw_ref[2:3, :] * u)
    st_out_ref[:, 0:a_w] = s1
    st_out_ref[:, a_w:2 * a_w] = u


def _conv_step(ps, layer, conv_state2, conv_w, a_w):
    bd = ps.shape[0]
    col = lambda c: pl.BlockSpec((bd, a_w), lambda i, c=c: (0, c))
    return pl.pallas_call(
        _conv_step_kernel,
        grid=(1,),
        in_specs=[col(0), col(1), col(2),
                  pl.BlockSpec((None, bd, 2 * a_w), lambda i: (layer, 0, 0)),
                  pl.BlockSpec((None, CONV_WIDTH, a_w), lambda i: (layer, 0, 0))],
        out_specs=[pl.BlockSpec((bd, a_w), lambda i: (0, 0)), pl.BlockSpec((bd, 2 * a_w), lambda i: (0, 0))],
        out_shape=[jax.ShapeDtypeStruct((bd, a_w), F32), jax.ShapeDtypeStruct((bd, 2 * a_w), F32)],
        compiler_params=_params("arbitrary"),
        name="conv_step",
    )(ps, ps, ps, conv_state2, conv_w)


def _pick_column(x, onehot):
    return jnp.sum(jnp.where(onehot, x, 0.0), axis=1, keepdims=True)


def _decode_unit(elem, qt, knt, vnt, sink, k_ref, v_ref, ko_ref, vo_ref, ot_old, lt_old, *, group, dil):
    n_kv, _, w = k_ref.shape
    bd = qt.shape[1]
    n_q = n_kv * group
    lane_b = lax.broadcasted_iota(jnp.int32, (1, bd), 1)
    row_b = lax.broadcasted_iota(jnp.int32, (bd, LANES), 0)
    lane_w = lax.broadcasted_iota(jnp.int32, (1, w), 1)
    key_ok = (lane_w % dil) == 0
    last = lane_w == w - 1
    head = lambda x, i: x[i * HEAD_DIM:(i + 1) * HEAD_DIM]

    onehot = lane_b == elem
    q_all = _dot(qt.astype(BF16), (row_b == elem).astype(BF16))
    kn_all = _pick_column(knt, onehot)
    vn_all = _pick_column(vnt, onehot)
    s_rows, sn_rows = [], []
    for h in range(n_kv):
        k = k_ref[h]
        kn = head(kn_all, h)
        ko_ref[h] = jnp.where(last, kn, pltpu.roll(k, w - 1, 1))
        for g in range(group):
            q = head(q_all, h * group + g)
            q_wide = jnp.tile(q, (1, w // LANES)) if w > LANES else q
            s_rows.append(jnp.sum(k * q_wide, axis=0, keepdims=True))
            sn_rows.append(jnp.sum(kn * q[:, 0:1], axis=0, keepdims=True))
    s = jnp.where(key_ok, jnp.concatenate(s_rows, axis=0) * SCALE, -jnp.inf)
    s_new = jnp.concatenate(sn_rows, axis=0) * SCALE
    m = jnp.maximum(jnp.max(s, axis=1, keepdims=True), s_new)
    if sink is not None:
        m = jnp.maximum(m, sink)
    e = jnp.exp(s - m)
    e_new = jnp.exp(s_new - m)
    den = jnp.sum(e, axis=1, keepdims=True) + e_new
    if sink is not None:
        den = den + jnp.exp(sink - m)
    p = e / den
    p_new = e_new / den
    o_cols = []
    for h in range(n_kv):
        v = v_ref[h]
        vn = head(vn_all, h)
        vo_ref[h] = jnp.where(last, vn, pltpu.roll(v, w - 1, 1))
        for g in range(group):
            hq = h * group + g
            o_cols.append(jnp.sum(v * p[hq:hq + 1, :], axis=1, keepdims=True) + p_new[hq:hq + 1, :] * vn)
    lse = m + jnp.log(den)
    if lt_old.shape[0] > n_q:
        lse = jnp.concatenate([lse, jnp.zeros((lt_old.shape[0] - n_q, 1), F32)], axis=0)
    return (jnp.where(onehot, jnp.concatenate(o_cols, axis=0), ot_old), jnp.where(onehot, lse, lt_old))


def _merge_kernel(*refs, alpha, n_lse, d):
    x_ref, gt_ref, ya_ref, yb_ref = refs[:4]
    pos = 4
    o_refs = refs[pos:pos + n_lse]
    l_refs = refs[pos + n_lse:pos + 2 * n_lse]
    pos += 2 * n_lse
    g1_ref, g2_ref, wa_ref, wb_ref, wc_ref, wo_ref, lg_ref, lb_ref, out_ref = refs[pos:]
    ls = [r[...] for r in l_refs]
    mx = functools.reduce(jnp.maximum, ls)
    ws = [jnp.exp(l - mx) for l in ls]
    tot = functools.reduce(lambda a, b: a + b, ws)
    yc = functools.reduce(lambda a, b: a + b, [(w / tot) * o[...] for w, o in zip(ws, o_refs)])
    hw = d // 2
    ga = g1_ref[:, 0:d]
    gb = jnp.concatenate([g1_ref[:, d:d + hw], g2_ref[:, 0:hw]], axis=1)
    gc = g2_ref[:, hw:hw + d]
    m = (jax.nn.sigmoid(ga) * _dot(ya_ref[...].astype(BF16), wa_ref[...])
         + jax.nn.sigmoid(gb) * _dot(yb_ref[...].astype(BF16), wb_ref[...])
         + jax.nn.sigmoid(gc) * _dot(yc.astype(BF16), wc_ref[...]))
    y = _dot(m.astype(BF16), wo_ref[...])
    z = alpha * x_ref[...] + gt_ref[...] * y
    out_ref[...] = _layer_norm(z, lg_ref[...], lb_ref[...])


def _merge(x, mod, layer, ya, yb, outs, lses, proj, g_off, wa, wb, wc, wo, ln_g, ln_b, *, tm, alpha):
    m, d = x.shape
    groups = mod.shape[1] if mod.ndim == 4 else 1
    tpg = (m // groups) // tm
    row = lambda w, c=0: pl.BlockSpec((tm, w), lambda i, c=c: (i, c))
    stacked = lambda a: pl.BlockSpec((None,) + a.shape[1:], lambda i: (layer, 0, 0))
    gw = 3 * d // 2
    in_specs = ([row(d)] + _mod_specs(mod, layer, (5,), tpg, d) + [row(ya.shape[1]), row(yb.shape[1])]
                + [row(o.shape[1]) for o in outs] + [row(l.shape[1]) for l in lses]
                + [row(gw, g_off // gw), row(gw, g_off // gw + 1), stacked(wa), stacked(wb), stacked(wc), stacked(wo),
                   _ln_spec(layer, 1, d), _ln_spec(layer, 1, d)])
    args = [x, mod, ya, yb] + list(outs) + list(lses) + [proj, proj, wa, wb, wc, wo, ln_g, ln_b]
    return pl.pallas_call(
        functools.partial(_merge_kernel, alpha=alpha, n_lse=len(outs), d=d),
        grid=(m // tm,),
        in_specs=in_specs,
        out_specs=row(d),
        out_shape=jax.ShapeDtypeStruct((m, d), F32),
        compiler_params=_params("arbitrary"),
        name="merge",
    )(*args)


def _keys_on_lanes(cache):
    return jnp.transpose(cache, (0, 1, 3, 4, 2))


def _keys_on_rows(cache_t):
    return jnp.transpose(cache_t, (0, 1, 4, 2, 3))


def kernel(x_prompt, x_sample, state_conv, cache_swa_k, cache_swa_v, cache_dil0_k, cache_dil0_v,
           cache_dil1_k, cache_dil1_v, cache_dil2_k, cache_dil2_v, c_prompt, c_sample, w_ada, b_ada,
           ln_g, ln_b, ffn_w_gate, ffn_w_up, ffn_w_down, w_in, conv_w, attn_sink,
           w_br_a, w_br_b, w_br_c, w_out):
    b, s, d = x_prompt.shape
    bd = x_sample.shape[0]
    depth = w_in.shape[0]
    a_w = w_br_a.shape[1]
    swa_q = w_br_b.shape[1]
    dil_w = w_br_c.shape[1]
    swa_kv = cache_swa_k.shape[3] * HEAD_DIM
    n_kv = swa_kv // HEAD_DIM
    swa_group = swa_q // swa_kv
    n_dh = dil_w // HEAD_DIM
    dq = len(DILATIONS) * dil_w
    alpha = (2.0 * depth) ** 0.25
    off_bq = 3 * a_w
    off_bk = off_bq + swa_q
    off_bv = off_bk + swa_kv
    off_cq = off_bv + swa_kv
    off_ck = off_cq + dq
    off_cv = off_ck + dq
    off_g = off_cv + dq
    f_dim = ffn_w_gate.shape[-1]

    tm_ffn = min(512, s)
    tm_proj = min(512, s)
    tf = _largest_tile(f_dim, 512)
    tf_host = _largest_tile(f_dim, 256)
    tn_in = _largest_tile(w_in.shape[-1], 1024)
    tn_in_prompt = _largest_tile(w_in.shape[-1], 3072)
    tn_ada = _largest_tile(w_ada.shape[-1], 1024)
    tm_merge = min(256, s)
    tm_conv = min(512, s)
    dil_heads_per_step = tuple(max(LANES // HEAD_DIM, n_dh // max(1, r // 8)) for r in DILATIONS)

    pad = (-(b + bd)) % SUBLANES
    c_rows = jnp.concatenate([c_sample, c_prompt, jnp.zeros((pad, d), F32)], axis=0)
    mod_s, mod_p = _ada(c_rows, bd, b, w_ada, b_ada, tn_ada)

    wg_b, wu_b, wd_b = _to_bf16(ffn_w_gate), _to_bf16(ffn_w_up), _to_bf16(ffn_w_down)
    w_in_b = _to_bf16(w_in)
    wa_b, wb_b, wc_b, wo_b = (_to_bf16(w) for w in (w_br_a, w_br_b, w_br_c, w_out))
    ln_g3 = ln_g.reshape(depth * N_SUB, 1, d)
    ln_b3 = ln_b.reshape(depth * N_SUB, 1, d)
    sink3 = attn_sink.reshape(depth, -1, 1)

    dec_groups = [(_keys_on_lanes(cache_swa_k), _keys_on_lanes(cache_swa_v), 1, (("proj", 0, bd),)),
                  (_keys_on_lanes(cache_dil0_k), _keys_on_lanes(cache_dil0_v), DILATIONS[0], (("proj", 0, bd),)),
                  (_keys_on_lanes(cache_dil1_k), _keys_on_lanes(cache_dil1_v), DILATIONS[1], (("swa", 0, bd),))]
    dec_acc = [(None, None)] * len(dec_groups)
    host_parts = 4
    host_heads = n_dh // host_parts
    host_shape = (depth, bd, host_parts, host_heads, HEAD_DIM, cache_dil2_k.shape[2])
    host_k = _keys_on_lanes(cache_dil2_k).reshape(host_shape)
    host_v = _keys_on_lanes(cache_dil2_v).reshape(host_shape)
    host_acc = (None, None)
    host_gi = len(DILATIONS) - 1
    conv2 = state_conv.reshape(depth, bd, 2 * a_w)

    xp = x_prompt.reshape(b * s, d)
    xs = x_sample.reshape(bd, d)
    new_p = [[] for _ in range(9)]
    conv_s = []
    for l in range(depth):
        ffn = lambda x, mod, sub, which, tm, host=None: _ffn(
            x, mod, l, sub, which, wg_b, wu_b, wd_b, ln_g3, ln_b3, tm=tm, tf=tf if host is None else tf_host,
            pair=1 if host is None else 2, alpha=alpha, res_w=0.5, host=host)
        xs = ffn(xs, mod_s, 0, 0, bd)
        ps = _proj(xs, mod_s, l, w_in_b, tm=bd, tn=tn_in)
        ya_s, st_s = _conv_step(ps, l, conv2, conv_w, a_w)
        conv_s.append(st_s.reshape(bd, CONV_WIDTH - 1, a_w))
        pst = ps[:, off_bq:off_g].T
        col = lambda off, width: pst[off - off_bq:off - off_bq + width]
        outs_s, lses_s = [0.0] * len(dec_groups), [0.0] * len(dec_groups)

        def carry(where):
            hosts = []
            for gidx, (kc, vc, dil, plan) in enumerate(dec_groups):
                for place, first, count in plan:
                    if place != where:
                        continue
                    if gidx == 0:
                        qt, knt, vnt, sink = col(off_bq, swa_q), col(off_bk, swa_kv), col(off_bv, swa_kv), sink3
                    else:
                        gi = gidx - 1
                        qt, knt, vnt, sink = (col(off_cq + gi * dil_w, dil_w), col(off_ck + gi * dil_w, dil_w),
                                              col(off_cv + gi * dil_w, dil_w), None)
                    hosts.append((gidx, _GroupHost(qt, knt, vnt, sink, kc, vc, *dec_acc[gidx], dil, first, count)))
            return hosts

        def collect(hosts, hosted):
            for n, (gidx, _) in enumerate(hosts):
                ko, vo, ot, lt = hosted[4 * n:4 * n + 4]
                dec_acc[gidx] = (ko, vo)
                outs_s[gidx] = outs_s[gidx] + ot.T
                lses_s[gidx] = lses_s[gidx] + jnp.repeat(lt.T, HEAD_DIM, axis=1)
        host_cols = [col(off + host_gi * dil_w, dil_w).reshape(host_parts, host_heads * HEAD_DIM, bd)
                     for off in (off_cq, off_ck, off_cv)]
        make_host = lambda first: _DecodeHost(*host_cols, host_k, host_v, *host_acc, l, first, bd // 2,
                                              DILATIONS[host_gi])
        xp, hk, hv, ot_a, lt_a = ffn(xp, mod_p, 0, 0, tm_ffn, make_host(0))
        host_acc = (hk, hv)
        hosts = carry("proj")
        pp, *hosted = _proj(xp, mod_p, l, w_in_b, tm=tm_proj, tn=tn_in_prompt,
                            hosts=tuple(h for _, h in hosts), elems=2)
        collect(hosts, hosted)
        p3 = pp.reshape(b, s, -1)
        ya_p, u_tail = _conv_prompt(p3, l, conv_w, a_w, tm_conv)
        hosts = carry("swa")
        yb_p, *hosted = _band_attention(p3, 1, off_bq, off_bk, off_bv, n_kv, swa_group, attn_sink[l], n_kv,
                                        hosts=tuple(h for _, h in hosts), elems=2, layer=l)
        collect(hosts, hosted)
        outs, lses = [], []
        for gi, r in enumerate(DILATIONS):
            og, lg = _band_attention(p3, r, off_cq + gi * dil_w, off_ck + gi * dil_w, off_cv + gi * dil_w,
                                     n_dh, 1, None, dil_heads_per_step[gi])
            outs.append(og.reshape(b * s, dil_w))
            lses.append(lg.reshape(b * s, dil_w))
        xp = _merge(xp, mod_p, l, ya_p.reshape(b * s, a_w), yb_p.reshape(b * s, swa_q), outs, lses, pp, off_g,
                    wa_b, wb_b, wc_b, wo_b, ln_g3, ln_b3, tm=tm_merge, alpha=alpha)
        xp, hk, hv, ot_b, lt_b = ffn(xp, mod_p, 2, 1, tm_ffn, make_host(bd // 2))
        host_acc = (hk, hv)
        outs_s.append((ot_a + ot_b).reshape(dil_w, bd).T)
        lt_h = (lt_a + lt_b)[:, :host_heads].reshape(n_dh, bd)
        lses_s.append(jnp.repeat(lt_h.T, HEAD_DIM, axis=1))
        xs = _merge(xs, mod_s, l, ya_s, outs_s[0], outs_s[1:], lses_s[1:], ps, off_g,
                    wa_b, wb_b, wc_b, wo_b, ln_g3, ln_b3, tm=bd, alpha=alpha)
        xs = ffn(xs, mod_s, 2, 1, bd)

        rows = min(SPAN, s)
        st_p = [u_tail[:, SUBLANES - (CONV_WIDTH - 1):],
                p3[:, s - rows:, off_bk:off_bk + swa_kv].reshape(b, rows, n_kv, HEAD_DIM),
                p3[:, s - rows:, off_bv:off_bv + swa_kv].reshape(b, rows, n_kv, HEAD_DIM)]
        for gi, r in enumerate(DILATIONS):
            rows = min(SPAN * r, s)
            for off in (off_ck, off_cv):
                c0 = off + gi * dil_w
                st_p.append(p3[:, s - rows:, c0:c0 + dil_w].reshape(b, rows, n_dh, HEAD_DIM))
        for acc, v in zip(new_p, st_p):
            acc.append(v)

    P = [jnp.stack(a, 0) for a in new_p]
    S = [jnp.stack(conv_s, 0)]
    for ko, vo in dec_acc:
        S += [_keys_on_rows(ko), _keys_on_rows(vo)]
    S += [_keys_on_rows(c.reshape(depth, bd, n_dh, HEAD_DIM, -1)) for c in host_acc]
    out = [xp.reshape(b, s, d), xs.reshape(bd, 1, d)]
    for pa, sa in zip(P, S):
        out += [pa, sa]
    return tuple(out)
```
